```python
import math
import jax, jax.numpy as jnp
from jax import lax
import numpy as np

D_MODEL = 1024
BATCH = 8
SEQ = 2048
DEPTH = 4
DEC_BATCH = 128
DEC_SEQ = 4
PAST_LEN = 16384
PAGE_SIZE = 128

N_MIXERS = 3
EXPAND = 2
D_INNER = EXPAND * D_MODEL
CHUNK = 128
SGU_GROUPS = 8
SGU_GROUP_DIM = D_INNER // SGU_GROUPS
CONV_W = 3
S5_IN = 16
S5_GROUPS = D_INNER // S5_IN
S5_STATE = 64
MEM_LEN = 256
XA_HEADS = 4
XA_HEAD_DIM = 128
XA_DIM = XA_HEADS * XA_HEAD_DIM
EPS = 1e-6

LAYER_KINDS = tuple(i % N_MIXERS for i in range(DEPTH))
KIND_INDEX = tuple(sum(1 for j in range(i) if LAYER_KINDS[j] == LAYER_KINDS[i]) for i in range(DEPTH))
N_A = sum(1 for k in LAYER_KINDS if k == 0)
N_B = sum(1 for k in LAYER_KINDS if k == 1)
N_C = sum(1 for k in LAYER_KINDS if k == 2)

kernel_name = "interleaved_sgu_conv_s5_memxattn_decoder_step"


def rmsnorm(x, g):
    xf = x.astype(jnp.float32)
    y = xf * lax.rsqrt(jnp.mean(xf * xf, axis=-1, keepdims=True) + EPS)
    return (y * g.astype(jnp.float32)).astype(x.dtype)


def memory_kv(mem, g, w_kv):
    b, m, _ = mem.shape
    kv = (rmsnorm(mem, g) @ w_kv).reshape(b, m, 2, XA_HEADS, XA_HEAD_DIM)
    return kv[:, :, 0], kv[:, :, 1]


def cross_attention(q, k, v):
    b, l, _ = q.shape
    qh = q.reshape(b, l, XA_HEADS, XA_HEAD_DIM)
    s = jnp.einsum("blhd,bmhd->bhlm", qh, k).astype(jnp.float32) * (XA_HEAD_DIM ** -0.5)
    p = jax.nn.softmax(s, axis=-1).astype(v.dtype)
    return jnp.einsum("bhlm,bmhd->blhd", p, v).reshape(b, l, XA_DIM)


def spatial_gate_prompt(u, v, ws, bs):
    b, l, _ = v.shape
    vc = v.reshape(b, l // CHUNK, CHUNK, SGU_GROUPS, SGU_GROUP_DIM)
    mixed = jnp.einsum("gts,bcsgd->bctgd", jnp.tril(ws), vc) + bs.T[None, None, :, :, None]
    return u * mixed.reshape(b, l, D_INNER)


def spatial_gate_sample(u, v, ws, bs):
    b, t, _ = v.shape
    vt = v.reshape(b, t, SGU_GROUPS, SGU_GROUP_DIM)
    mixed = jnp.einsum("gts,bsgd->btgd", jnp.tril(ws[:, :t, :t]), vt) + bs[:, :t].T[None, :, :, None]
    return u * mixed.reshape(b, t, D_INNER)


def short_conv(xc, buf, w):
    l = xc.shape[1]
    xp = jnp.concatenate([buf, xc], axis=1)
    y = sum(w[k] * xp[:, k:k + l] for k in range(CONV_W))
    return y, xp[:, l:]


def s5_discretize(lam_re, lam_im, log_dt, b_re, b_im):
    dt = jnp.exp(log_dt)[:, None]
    mag = jnp.exp(lam_re * dt)
    ar = mag * jnp.cos(lam_im * dt)
    ai = mag * jnp.sin(lam_im * dt)
    nr, ni = ar - 1, ai
    den = lam_re * lam_re + lam_im * lam_im
    cr = (nr * lam_re + ni * lam_im) / den
    ci = (ni * lam_re - nr * lam_im) / den
    bbr = cr[..., None] * b_re - ci[..., None] * b_im
    bbi = cr[..., None] * b_im + ci[..., None] * b_re
    return ar, ai, bbr, bbi


def _complex_affine_combine(e1, e2):
    a1r, a1i, b1r, b1i = e1
    a2r, a2i, b2r, b2i = e2
    return (a1r * a2r - a1i * a2i,
            a1r * a2i + a1i * a2r,
            a2r * b1r - a2i * b1i + b2r,
            a2r * b1i + a2i * b1r + b2i)


def s5_block(u, h0r, h0i, ar, ai, bbr, bbi, c_re, c_im):
    bur = jnp.einsum("gpi,blgi->blgp", bbr, u)
    bui = jnp.einsum("gpi,blgi->blgp", bbi, u)
    bur = bur.at[:, 0].add(ar * h0r - ai * h0i)
    bui = bui.at[:, 0].add(ar * h0i + ai * h0r)
    a_r = jnp.broadcast_to(ar, bur.shape)
    a_i = jnp.broadcast_to(ai, bui.shape)
    _, _, hr, hi = lax.associative_scan(_complex_affine_combine, (a_r, a_i, bur, bui), axis=1)
    y = jnp.einsum("gip,blgp->blgi", c_re, hr) - jnp.einsum("gip,blgp->blgi", c_im, hi)
    return y, hr[:, -1].astype(h0r.dtype), hi[:, -1].astype(h0i.dtype)


def s5_mixer(u, h0r, h0i, lam_re, lam_im, log_dt, b_re, b_im, c_re, c_im, d, w_glu, b_glu, blocked):
    b, l, _ = u.shape
    ar, ai, bbr, bbi = s5_discretize(lam_re, lam_im, log_dt, b_re, b_im)
    ug = u.reshape(b, l, S5_GROUPS, S5_IN)
    if blocked:
        uc = ug.reshape(b, l // CHUNK, CHUNK, S5_GROUPS, S5_IN).transpose(1, 0, 2, 3, 4)

        def step(carry, u_blk):
            y_blk, hr_, hi_ = s5_block(u_blk, carry[0], carry[1], ar, ai, bbr, bbi, c_re, c_im)
            return (hr_, hi_), y_blk

        (hr, hi), ys = lax.scan(step, (h0r, h0i), uc)
        y = ys.transpose(1, 0, 2, 3, 4).reshape(b, l, D_INNER)
    else:
        y, hr, hi = s5_block(ug, h0r, h0i, ar, ai, bbr, bbi, c_re, c_im)
        y = y.reshape(b, l, D_INNER)
    y = y + d * u
    g = jax.nn.gelu(y)
    return g * jax.nn.sigmoid(g @ w_glu + b_glu), hr, hi


def setup_inputs(seed: int = 0) -> dict:
    key = jax.random.key(seed)
    ks = iter(jax.random.split(key, 40))
    f32 = jnp.float32
    nrm = lambda shape, scale: jax.random.normal(next(ks), shape, f32) * scale
    x_prompt = nrm((BATCH, SEQ, D_MODEL), 1.0)
    x_sample = nrm((DEC_BATCH, DEC_SEQ, D_MODEL), 1.0)
    mem_prompt = nrm((BATCH, MEM_LEN, D_MODEL), 1.0)
    cache_mem_k = nrm((DEPTH, DEC_BATCH, MEM_LEN, XA_HEADS, XA_HEAD_DIM), 1.0)
    cache_mem_v = nrm((DEPTH, DEC_BATCH, MEM_LEN, XA_HEADS, XA_HEAD_DIM), 1.0)
    state_conv = nrm((N_B, DEC_BATCH, CONV_W - 1, D_INNER), 1.0)
    state_s5_re = nrm((N_C, DEC_BATCH, S5_GROUPS, S5_STATE), 0.5)
    state_s5_im = nrm((N_C, DEC_BATCH, S5_GROUPS, S5_STATE), 0.5)
    norm_g = 1.0 + nrm((DEPTH, D_MODEL), 0.1)
    final_g = 1.0 + nrm((D_MODEL,), 0.1)
    mem_norm_g = 1.0 + nrm((DEPTH, D_MODEL), 0.1)
    w_kv = nrm((DEPTH, D_MODEL, 2 * XA_DIM), D_MODEL ** -0.5)
    w_out = nrm((DEPTH, D_INNER + XA_DIM, D_MODEL), 0.5 * (D_INNER + XA_DIM) ** -0.5)
    w_in_a = nrm((N_A, D_MODEL, 3 * D_INNER + XA_DIM), D_MODEL ** -0.5)
    sgu_norm_g = 1.0 + nrm((N_A, D_INNER), 0.1)
    sgu_w = nrm((N_A, SGU_GROUPS, CHUNK, CHUNK), CHUNK ** -0.5)
    sgu_b = 1.0 + nrm((N_A, SGU_GROUPS, CHUNK), 0.1)
    w_in_b = nrm((N_B, D_MODEL, 4 * D_INNER + XA_DIM), D_MODEL ** -0.5)
    conv_w = nrm((N_B, CONV_W, D_INNER), CONV_W ** -0.5)
    w_in_c = nrm((N_C, D_MODEL, 2 * D_INNER + XA_DIM), D_MODEL ** -0.5)
    s5_lam_re = -0.5 + nrm((N_C, S5_GROUPS, S5_STATE), 0.01)
    s5_lam_im = jnp.pi * jnp.arange(S5_STATE, dtype=f32)[None, None, :] + nrm((N_C, S5_GROUPS, S5_STATE), 0.01)
    s5_log_dt = jax.random.uniform(next(ks), (N_C, S5_GROUPS), f32, minval=math.log(1e-3), maxval=math.log(1e-1))
    s5_b_re = nrm((N_C, S5_GROUPS, S5_STATE, S5_IN), (2 * S5_IN) ** -0.5)
    s5_b_im = nrm((N_C, S5_GROUPS, S5_STATE, S5_IN), (2 * S5_IN) ** -0.5)
    s5_c_re = nrm((N_C, S5_GROUPS, S5_IN, S5_STATE), (2 * S5_STATE) ** -0.5)
    s5_c_im = nrm((N_C, S5_GROUPS, S5_IN, S5_STATE), (2 * S5_STATE) ** -0.5)
    s5_d = nrm((N_C, D_INNER), 1.0)
    w_glu = nrm((N_C, D_INNER, D_INNER), D_INNER ** -0.5)
    b_glu = nrm((N_C, D_INNER), 0.01)
    return {"x_prompt": x_prompt, "x_sample": x_sample, "mem_prompt": mem_prompt,
            "cache_mem_k": cache_mem_k, "cache_mem_v": cache_mem_v, "state_conv": state_conv,
            "state_s5_re": state_s5_re, "state_s5_im": state_s5_im,
            "norm_g": norm_g, "final_g": final_g, "mem_norm_g": mem_norm_g, "w_kv": w_kv, "w_out": w_out,
            "w_in_a": w_in_a, "sgu_norm_g": sgu_norm_g, "sgu_w": sgu_w, "sgu_b": sgu_b,
            "w_in_b": w_in_b, "conv_w": conv_w, "w_in_c": w_in_c,
            "s5_lam_re": s5_lam_re, "s5_lam_im": s5_lam_im, "s5_log_dt": s5_log_dt,
            "s5_b_re": s5_b_re, "s5_b_im": s5_b_im, "s5_c_re": s5_c_re, "s5_c_im": s5_c_im,
            "s5_d": s5_d, "w_glu": w_glu, "b_glu": b_glu}


def reference(x_prompt, x_sample, mem_prompt, cache_mem_k, cache_mem_v, state_conv, state_s5_re, state_s5_im,
              norm_g, final_g, mem_norm_g, w_kv, w_out, w_in_a, sgu_norm_g, sgu_w, sgu_b, w_in_b, conv_w, w_in_c,
              s5_lam_re, s5_lam_im, s5_log_dt, s5_b_re, s5_b_im, s5_c_re, s5_c_im, s5_d, w_glu, b_glu):
    E = D_INNER

    def trunk(x, mem_k, mem_v, conv_bufs, s5_re, s5_im, prompt):
        conv_out, s5r_out, s5i_out, chunk_v_out = [], [], [], []
        for i in range(DEPTH):
            kind, j = LAYER_KINDS[i], KIND_INDEX[i]
            h = rmsnorm(x, norm_g[i])
            if kind == 0:
                u, v, z, q = jnp.split(h @ w_in_a[j], [E, 2 * E, 3 * E], axis=-1)
                v = rmsnorm(v, sgu_norm_g[j])
                if prompt:
                    y = spatial_gate_prompt(u, v, sgu_w[j], sgu_b[j])
                else:
                    y = spatial_gate_sample(u, v, sgu_w[j], sgu_b[j])
                    chunk_v_out.append(v)
            elif kind == 1:
                bg, cg, hv, z, q = jnp.split(h @ w_in_b[j], [E, 2 * E, 3 * E, 4 * E], axis=-1)
                y, buf = short_conv(cg * hv, conv_bufs[j], conv_w[j])
                y = bg * y
                conv_out.append(buf)
            else:
                u, z, q = jnp.split(h @ w_in_c[j], [E, 2 * E], axis=-1)
                y, hr, hi = s5_mixer(u, s5_re[j], s5_im[j], s5_lam_re[j], s5_lam_im[j], s5_log_dt[j],
                                     s5_b_re[j], s5_b_im[j], s5_c_re[j], s5_c_im[j], s5_d[j],
                                     w_glu[j], b_glu[j], prompt)
                s5r_out.append(hr)
                s5i_out.append(hi)
            xa = cross_attention(q, mem_k[i], mem_v[i])
            x = x + jnp.concatenate([y * jax.nn.silu(z), xa], axis=-1) @ w_out[i]
        return rmsnorm(x, final_g), conv_out, s5r_out, s5i_out, chunk_v_out

    kv = [memory_kv(mem_prompt, mem_norm_g[i], w_kv[i]) for i in range(DEPTH)]
    mem_k_prompt = jnp.stack([k for k, _ in kv])
    mem_v_prompt = jnp.stack([v for _, v in kv])
    bp = x_prompt.shape[0]
    conv0 = jnp.zeros((N_B, bp, CONV_W - 1, D_INNER), x_prompt.dtype)
    s50 = jnp.zeros((N_C, bp, S5_GROUPS, S5_STATE), x_prompt.dtype)
    y_prompt, conv_p, s5r_p, s5i_p, _ = trunk(x_prompt, mem_k_prompt, mem_v_prompt, conv0, s50, s50, True)

    y_sample, conv_s, s5r_s, s5i_s, cv_s = trunk(x_sample, cache_mem_k, cache_mem_v, state_conv,
                                                 state_s5_re, state_s5_im, False)
    return (y_prompt, y_sample, mem_k_prompt, mem_v_prompt, jnp.stack(conv_p), jnp.stack(conv_s),
            jnp.stack(s5r_p), jnp.stack(s5i_p), jnp.stack(s5r_s), jnp.stack(s5i_s), jnp.stack(cv_s))
```

```python
import functools

import jax
import jax.numpy as jnp
from jax import lax
from jax.experimental import pallas as pl
from jax.experimental.pallas import tpu as pltpu

F32 = jnp.float32
BF16 = jnp.bfloat16

EPS = 1e-6
N_MIXERS = 3
CHUNK = 128
SGU_GROUPS = 8
CONV_W = 3
S5_IN = 16
XA_HEADS = 4
XA_HEAD_DIM = 128
XA_DIM = XA_HEADS * XA_HEAD_DIM

V7X_SUBLANES = 8
V7X_VMEM_LIMIT_BYTES = 56 * 1024 * 1024
S5_FEATS_PER_BLOCK = 256
S5_SCAN_LANES = 512


def _params(*semantics):
    return pltpu.CompilerParams(dimension_semantics=semantics, vmem_limit_bytes=V7X_VMEM_LIMIT_BYTES)


def _resident(shape):
    nd = len(shape)
    return pl.BlockSpec(shape, lambda *_: (0,) * nd, pipeline_mode=pl.Buffered(1))


def _rms(x, g):
    return x * lax.rsqrt(jnp.mean(x * x, axis=-1, keepdims=True) + EPS) * g


def _silu(z):
    return z * jax.nn.sigmoid(z)


def _norm_proj_kernel(x_ref, g_ref, w_ref, o_ref, hn_ref):
    @pl.when(pl.program_id(1) == 0)
    def _():
        hn_ref[...] = _rms(x_ref[...], g_ref[...]).astype(BF16)

    o_ref[...] = jnp.dot(hn_ref[...], w_ref[...], preferred_element_type=F32)


def norm_proj(x, g, w, *, tm, tn=512):
    m, d = x.shape
    n = w.shape[1]
    tm = min(tm, m)
    return pl.pallas_call(
        _norm_proj_kernel,
        grid=(m // tm, n // tn),
        in_specs=[pl.BlockSpec((tm, d), lambda i, j: (i, 0)),
                  pl.BlockSpec((1, d), lambda i, j: (0, 0)),
                  pl.BlockSpec((d, tn), lambda i, j: (0, j))],
        out_specs=pl.BlockSpec((tm, tn), lambda i, j: (i, j)),
        out_shape=jax.ShapeDtypeStruct((m, n), F32),
        scratch_shapes=[pltpu.VMEM((tm, d), BF16)],
        compiler_params=_params("parallel", "arbitrary"),
        name="norm_proj",
    )(x, g.reshape(1, d), w)


def _mem_kv_kernel(x_ref, g_ref, w_ref, k_ref, v_ref, kb_ref, vb_ref):
    hn = _rms(x_ref[...], g_ref[0]).astype(BF16)
    kv = jnp.dot(hn, w_ref[0], preferred_element_type=F32)
    k, v = kv[:, :XA_DIM], kv[:, XA_DIM:]
    k_ref[0] = k
    v_ref[0] = v
    kb_ref[0] = k.astype(BF16)
    vb_ref[0] = v.astype(BF16)


def mem_kv(mem, g, w, *, tm=512):
    m, d = mem.shape
    depth = g.shape[0]
    out = pl.BlockSpec((1, tm, XA_DIM), lambda l, i: (l, i, 0))
    shape = lambda dt: jax.ShapeDtypeStruct((depth, m, XA_DIM), dt)
    return pl.pallas_call(
        _mem_kv_kernel,
        grid=(depth, m // tm),
        in_specs=[pl.BlockSpec((tm, d), lambda l, i: (i, 0)),
                  pl.BlockSpec((1, 1, d), lambda l, i: (l, 0, 0)),
                  pl.BlockSpec((1, d, 2 * XA_DIM), lambda l, i: (l, 0, 0))],
        out_specs=[out, out, out, out],
        out_shape=[shape(F32), shape(F32), shape(BF16), shape(BF16)],
        compiler_params=_params("parallel", "parallel"),
        name="mem_kv",
    )(mem, g.reshape(depth, 1, d), w)


def _attn_kernel(q_ref, k_ref, v_ref, o_ref, *, nb):
    scale = XA_HEAD_DIM ** -0.5
    for b in range(nb):
        q = q_ref[b].astype(BF16)
        k = k_ref[b].astype(BF16)
        v = v_ref[b].astype(BF16)
        for h in range(XA_HEADS):
            hs = slice(h * XA_HEAD_DIM, (h + 1) * XA_HEAD_DIM)
            s = lax.dot_general(q[:, hs], k[:, hs], (((1,), (1,)), ((), ())),
                                preferred_element_type=F32) * scale
            e = jnp.exp(s - jnp.max(s, axis=-1, keepdims=True))
            p = e / jnp.sum(e, axis=-1, keepdims=True)
            o = jnp.dot(p.astype(BF16), v[:, hs], preferred_element_type=F32)
            o_ref[b, :, hs] = o.astype(o_ref.dtype)


def cross_attention(q, qcol, k, v, *, nb, tq, out_dtype):
    b, l, _ = q.shape
    mem = k.shape[1]
    return pl.pallas_call(
        functools.partial(_attn_kernel, nb=nb),
        grid=(b // nb, l // tq),
        in_specs=[pl.BlockSpec((nb, tq, XA_DIM), lambda i, j: (i, j, qcol)),
                  pl.BlockSpec((nb, mem, XA_DIM), lambda i, j: (i, 0, 0)),
                  pl.BlockSpec((nb, mem, XA_DIM), lambda i, j: (i, 0, 0))],
        out_specs=pl.BlockSpec((nb, tq, XA_DIM), lambda i, j: (i, j, 0)),
        out_shape=jax.ShapeDtypeStruct((b, l, XA_DIM), out_dtype),
        compiler_params=_params("parallel", "arbitrary"),
        name="cross_attention",
    )(q, k, v)


def _out_proj_kernel(y_ref, xa_ref, w1_ref, w2_ref, x_ref, fg_ref, o_ref, *, final_norm):
    acc = x_ref[...] + jnp.dot(y_ref[...], w1_ref[...], preferred_element_type=F32)
    acc = acc + jnp.dot(xa_ref[...], w2_ref[...], preferred_element_type=F32)
    o_ref[...] = _rms(acc, fg_ref[...]) if final_norm else acc


def out_proj(y, xa, w1, w2, x, final_g, *, final_norm, tm=512):
    m, d = x.shape
    e, a = y.shape[1], xa.shape[1]
    tm = min(tm, m)
    return pl.pallas_call(
        functools.partial(_out_proj_kernel, final_norm=final_norm),
        grid=(m // tm,),
        in_specs=[pl.BlockSpec((tm, e), lambda i: (i, 0)),
                  pl.BlockSpec((tm, a), lambda i: (i, 0)),
                  _resident((e, d)),
                  _resident((a, d)),
                  pl.BlockSpec((tm, d), lambda i: (i, 0)),
                  _resident((1, d))],
        out_specs=pl.BlockSpec((tm, d), lambda i: (i, 0)),
        out_shape=jax.ShapeDtypeStruct((m, d), F32),
        compiler_params=_params("parallel"),
        name="out_proj",
    )(y, xa, w1, w2, x, final_g.reshape(1, d))


def _sgu_prompt_kernel(u_ref, v_ref, z_ref, g_ref, ws_ref, bias_ref, o_ref, *, chunks):
    gd = u_ref.shape[1] // SGU_GROUPS
    row = lax.broadcasted_iota(jnp.int32, (CHUNK, CHUNK), 0)
    col = lax.broadcasted_iota(jnp.int32, (CHUNK, CHUNK), 1)
    ws = [jnp.where(row >= col, ws_ref[g], 0.0).astype(BF16) for g in range(SGU_GROUPS)]
    for c in range(chunks):
        rs = slice(c * CHUNK, (c + 1) * CHUNK)
        vn = _rms(v_ref[rs, :], g_ref[...]).astype(BF16)
        for g in range(SGU_GROUPS):
            cs = slice(g * gd, (g + 1) * gd)
            mixed = jnp.dot(ws[g], vn[:, cs], preferred_element_type=F32) + bias_ref[:, cs]
            o_ref[rs, cs] = (u_ref[rs, cs] * mixed * _silu(z_ref[rs, cs])).astype(BF16)


def sgu_prompt(proj, g, ws, bias, *, e, chunks=2):
    m = proj.shape[0]
    tm = chunks * CHUNK
    col = lambda c: pl.BlockSpec((tm, e), lambda i: (i, c))
    return pl.pallas_call(
        functools.partial(_sgu_prompt_kernel, chunks=chunks),
        grid=(m // tm,),
        in_specs=[col(0), col(1), col(2), _resident((1, e)), _resident(ws.shape), _resident(bias.shape)],
        out_specs=pl.BlockSpec((tm, e), lambda i: (i, 0)),
        out_shape=jax.ShapeDtypeStruct((m, e), BF16),
        compiler_params=_params("parallel"),
        name="sgu_prompt",
    )(proj, proj, proj, g.reshape(1, e), ws, bias)


def _sgu_sample_kernel(u_ref, v_ref, z_ref, g_ref, coef_ref, bias_ref, o_ref, vn_ref):
    steps = u_ref.shape[0]
    vn = []
    for t in range(steps):
        n = _rms(v_ref[t], g_ref[...])
        vn_ref[t] = n
        vn.append(n)
    for t in range(steps):
        mixed = bias_ref[t:t + 1, :]
        for s in range(t + 1):
            mixed = mixed + coef_ref[t * steps + s:t * steps + s + 1, :] * vn[s]
        o_ref[t] = (u_ref[t] * mixed * _silu(z_ref[t])).astype(BF16)


def sgu_sample(proj, g, coef, bias, *, e, nbb=32):
    t, b, _ = proj.shape
    col = lambda c: pl.BlockSpec((t, nbb, e), lambda i: (0, i, c))
    out = pl.BlockSpec((t, nbb, e), lambda i: (0, i, 0))
    return pl.pallas_call(
        _sgu_sample_kernel,
        grid=(b // nbb,),
        in_specs=[col(0), col(1), col(2), _resident((1, e)), _resident(coef.shape), _resident(bias.shape)],
        out_specs=[out, out],
        out_shape=[jax.ShapeDtypeStruct((t, b, e), BF16), jax.ShapeDtypeStruct((t, b, e), F32)],
        compiler_params=_params("parallel"),
        name="sgu_sample",
    )(proj, proj, proj, g.reshape(1, e), coef, bias)


CONV_EDGE_ROWS = 16


def _conv_prompt_kernel(bg_ref, cg_ref, hv_ref, z_ref, w_ref, init_ref, o_ref, st_ref, carry_ref):
    i = pl.program_id(1)
    tm = cg_ref.shape[0]
    edge = CONV_EDGE_ROWS

    @pl.when(i == 0)
    def _():
        carry_ref[...] = jnp.zeros_like(carry_ref)
        carry_ref[edge - (CONV_W - 1):edge, :] = init_ref[0]

    w0, w1, w2 = w_ref[0:1, :], w_ref[1:2, :], w_ref[2:3, :]
    xc = cg_ref[...] * hv_ref[...]
    r1 = pltpu.roll(xc, 1, 0)
    r2 = pltpu.roll(xc, 2, 0)
    y = w0 * r2 + w1 * r1 + w2 * xc
    o_ref[...] = (bg_ref[...] * y * _silu(z_ref[...])).astype(BF16)

    prev = carry_ref[...]
    rid = lax.broadcasted_iota(jnp.int32, (edge, xc.shape[1]), 0)
    x1 = jnp.where(rid < 1, pltpu.roll(prev, 1, 0), r1[:edge])
    x2 = jnp.where(rid < 2, pltpu.roll(prev, 2, 0), r2[:edge])
    ye = w0 * x2 + w1 * x1 + w2 * xc[:edge]
    o_ref[0:edge, :] = (bg_ref[0:edge, :] * ye * _silu(z_ref[0:edge, :])).astype(BF16)

    carry_ref[...] = xc[tm - edge:, :]
    st_ref[0] = xc[tm - (CONV_W - 1):, :]


def conv_prompt(proj, w, init, *, e, b, tm=256):
    m = proj.shape[0]
    lt = m // b // tm
    col = lambda c: pl.BlockSpec((tm, e), lambda bi, i: (bi * lt + i, c))
    return pl.pallas_call(
        _conv_prompt_kernel,
        grid=(b, lt),
        in_specs=[col(0), col(1), col(2), col(3), _resident((CONV_W, e)),
                  pl.BlockSpec((1, CONV_W - 1, e), lambda bi, i: (bi, 0, 0))],
        out_specs=[pl.BlockSpec((tm, e), lambda bi, i: (bi * lt + i, 0)),
                   pl.BlockSpec((1, CONV_W - 1, e), lambda bi, i: (bi, 0, 0))],
        out_shape=[jax.ShapeDtypeStruct((m, e), BF16), jax.ShapeDtypeStruct((b, CONV_W - 1, e), F32)],
        scratch_shapes=[pltpu.VMEM((CONV_EDGE_ROWS, e), F32)],
        compiler_params=_params("parallel", "arbitrary"),
        name="conv_prompt",
    )(proj, proj, proj, proj, w, init)


def _conv_sample_kernel(bg_ref, cg_ref, hv_ref, z_ref, w_ref, st_ref, o_ref, ns_ref):
    steps = cg_ref.shape[0]
    xp = [st_ref[k] for k in range(CONV_W - 1)] + [cg_ref[t] * hv_ref[t] for t in range(steps)]
    for t in range(steps):
        y = w_ref[0:1, :] * xp[t]
        for k in range(1, CONV_W):
            y = y + w_ref[k:k + 1, :] * xp[t + k]
        o_ref[t] = (bg_ref[t] * y * _silu(z_ref[t])).astype(BF16)
    for k in range(CONV_W - 1):
        ns_ref[k] = xp[steps + k]


def conv_sample(proj, w, state, *, e, nbb=32):
    t, b, _ = proj.shape
    col = lambda c: pl.BlockSpec((t, nbb, e), lambda i: (0, i, c))
    st = pl.BlockSpec((CONV_W - 1, nbb, e), lambda i: (0, i, 0))
    return pl.pallas_call(
        _conv_sample_kernel,
        grid=(b // nbb,),
        in_specs=[col(0), col(1), col(2), col(3), _resident((CONV_W, e)), st],
        out_specs=[pl.BlockSpec((t, nbb, e), lambda i: (0, i, 0)), st],
        out_shape=[jax.ShapeDtypeStruct((t, b, e), BF16), jax.ShapeDtypeStruct((CONV_W - 1, b, e), F32)],
        compiler_params=_params("parallel"),
        name="conv_sample",
    )(proj, proj, proj, proj, w, state)


def _s5_disc_kernel(lre_ref, lim_ref, ldt_ref, bre_ref, bim_ref, ar_ref, ai_ref, bbr_ref, bbi_ref):
    dt = jnp.exp(ldt_ref[...])
    lr, li = lre_ref[...], lim_ref[...]
    mag = jnp.exp(lr * dt)
    ar = mag * jnp.cos(li * dt)
    ai = mag * jnp.sin(li * dt)
    nr, ni = ar - 1, ai
    den = lr * lr + li * li
    cr = (nr * lr + ni * li) / den
    ci = (ni * lr - nr * li) / den
    ar_ref[...] = ar
    ai_ref[...] = ai
    cr3, ci3 = cr[:, None, :], ci[:, None, :]
    bbr_ref[...] = cr3 * bre_ref[...] - ci3 * bim_ref[...]
    bbi_ref[...] = cr3 * bim_ref[...] + ci3 * bre_ref[...]


def s5_discretize(lam_re, lam_im, log_dt, b_re, b_im):
    g, p = lam_re.shape
    gp = jax.ShapeDtypeStruct((g, p), F32)
    gip = jax.ShapeDtypeStruct(b_re.shape, F32)
    return pl.pallas_call(
        _s5_disc_kernel,
        out_shape=[gp, gp, gip, gip],
        compiler_params=pltpu.CompilerParams(vmem_limit_bytes=V7X_VMEM_LIMIT_BYTES),
        name="s5_discretize",
    )(lam_re, lam_im, log_dt.reshape(g, 1), b_re, b_im)


def _s5_kernel(u_ref, bre_ref, bim_ref, cre_ref, cim_ref, ar_ref, ai_ref, d_ref, h0r_ref, h0i_ref,
               g_ref, hr_ref, hi_ref, sr_ref, si_ref):
    tl, nb, e = u_ref.shape
    rows = tl * nb
    n_blk = bre_ref.shape[0]
    fk = bre_ref.shape[1]
    sk = bre_ref.shape[2]
    lanes = sr_ref.shape[1]

    @pl.when(pl.program_id(1) == 0)
    def _():
        hr_ref[...] = h0r_ref[...]
        hi_ref[...] = h0i_ref[...]

    u = u_ref[...].reshape(rows, e)
    ub = u.astype(BF16)
    for kb in range(n_blk):
        uk = ub[:, kb * fk:(kb + 1) * fk]
        sr_ref[:, kb * sk:(kb + 1) * sk] = jnp.dot(uk, bre_ref[kb], preferred_element_type=F32)
        si_ref[:, kb * sk:(kb + 1) * sk] = jnp.dot(uk, bim_ref[kb], preferred_element_type=F32)

    for c in range(lanes // S5_SCAN_LANES):
        ls = slice(c * S5_SCAN_LANES, (c + 1) * S5_SCAN_LANES)
        a_r = jnp.broadcast_to(ar_ref[:, ls], (V7X_SUBLANES, S5_SCAN_LANES))
        a_i = jnp.broadcast_to(ai_ref[:, ls], (V7X_SUBLANES, S5_SCAN_LANES))

        def sweep(sg, _):
            r0 = pl.multiple_of(sg * V7X_SUBLANES, V7X_SUBLANES)

            def step(t, h):
                hr, hi = h
                row = pl.multiple_of(t * nb + r0, V7X_SUBLANES)
                nr = a_r * hr - a_i * hi + sr_ref[pl.ds(row, V7X_SUBLANES), ls]
                ni = a_r * hi + a_i * hr + si_ref[pl.ds(row, V7X_SUBLANES), ls]
                sr_ref[pl.ds(row, V7X_SUBLANES), ls] = nr
                si_ref[pl.ds(row, V7X_SUBLANES), ls] = ni
                return nr, ni

            h0 = (hr_ref[pl.ds(r0, V7X_SUBLANES), ls], hi_ref[pl.ds(r0, V7X_SUBLANES), ls])
            hr, hi = lax.fori_loop(0, tl, step, h0, unroll=min(tl, 8))
            hr_ref[pl.ds(r0, V7X_SUBLANES), ls] = hr
            hi_ref[pl.ds(r0, V7X_SUBLANES), ls] = hi
            return 0

        lax.fori_loop(0, nb // V7X_SUBLANES, sweep, 0)

    for kb in range(n_blk):
        hrb = sr_ref[:, kb * sk:(kb + 1) * sk].astype(BF16)
        hib = si_ref[:, kb * sk:(kb + 1) * sk].astype(BF16)
        y = jnp.dot(hrb, cre_ref[kb], preferred_element_type=F32)
        y = y - jnp.dot(hib, cim_ref[kb], preferred_element_type=F32)
        fs = slice(kb * fk, (kb + 1) * fk)
        y = y + d_ref[:, fs] * u[:, fs]
        g_ref[:, :, fs] = jax.nn.gelu(y).reshape(tl, nb, fk)


def s5_scan(proj, bre, bim, cre, cim, ar, ai, d, h0r, h0i, *, e, tl, nbb):
    l, b, _ = proj.shape
    lanes = h0r.shape[1]
    rows = tl * nbb
    st = pl.BlockSpec((nbb, lanes), lambda bi, i: (bi, 0))
    return pl.pallas_call(
        _s5_kernel,
        grid=(b // nbb, l // tl),
        in_specs=[pl.BlockSpec((tl, nbb, e), lambda bi, i: (i, bi, 0)),
                  _resident(bre.shape), _resident(bim.shape), _resident(cre.shape), _resident(cim.shape),
                  _resident((1, lanes)), _resident((1, lanes)), _resident((1, e)), st, st],
        out_specs=[pl.BlockSpec((tl, nbb, e), lambda bi, i: (i, bi, 0)), st, st],
        out_shape=[jax.ShapeDtypeStruct((l, b, e), F32),
                   jax.ShapeDtypeStruct((b, lanes), F32), jax.ShapeDtypeStruct((b, lanes), F32)],
        scratch_shapes=[pltpu.VMEM((rows, lanes), F32), pltpu.VMEM((rows, lanes), F32)],
        compiler_params=_params("parallel", "arbitrary"),
        name="s5_scan",
    )(proj, bre, bim, cre, cim, ar.reshape(1, lanes), ai.reshape(1, lanes), d.reshape(1, e), h0r, h0i)


def _glu_kernel(g_ref, w_ref, b_ref, gj_ref, z_ref, o_ref, gb_ref):
    @pl.when(pl.program_id(1) == 0)
    def _():
        gb_ref[...] = g_ref[...].astype(BF16)

    a = jnp.dot(gb_ref[...], w_ref[...], preferred_element_type=F32) + b_ref[...]
    o_ref[...] = (gj_ref[...] * jax.nn.sigmoid(a) * _silu(z_ref[...])).astype(BF16)


def glu_gate(g, w, bias, proj, *, e, tm=512, tn=512):
    m = g.shape[0]
    tm = min(tm, m)
    return pl.pallas_call(
        _glu_kernel,
        grid=(m // tm, e // tn),
        in_specs=[pl.BlockSpec((tm, e), lambda i, j: (i, 0)),
                  pl.BlockSpec((e, tn), lambda i, j: (0, j)),
                  pl.BlockSpec((1, tn), lambda i, j: (0, j)),
                  pl.BlockSpec((tm, tn), lambda i, j: (i, j)),
                  pl.BlockSpec((tm, tn), lambda i, j: (i, e // tn + j))],
        out_specs=pl.BlockSpec((tm, tn), lambda i, j: (i, j)),
        out_shape=jax.ShapeDtypeStruct((m, e), BF16),
        scratch_shapes=[pltpu.VMEM((tm, e), BF16)],
        compiler_params=_params("parallel", "arbitrary"),
        name="glu_gate",
    )(g, w, bias.reshape(1, e), g, proj)


def _block_diag(x, n_blk):
    g, r, c = x.shape
    gb = g // n_blk
    eye = jnp.eye(gb, dtype=x.dtype)
    out = jnp.einsum("bgrc,gh->bgrhc", x.reshape(n_blk, gb, r, c), eye)
    return out.reshape(n_blk, gb * r, gb * c)


def kernel(x_prompt, x_sample, mem_prompt, cache_mem_k, cache_mem_v, state_conv, state_s5_re, state_s5_im,
           norm_g, final_g, mem_norm_g, w_kv, w_out, w_in_a, sgu_norm_g, sgu_w, sgu_b, w_in_b, conv_w, w_in_c,
           s5_lam_re, s5_lam_im, s5_log_dt, s5_b_re, s5_b_im, s5_c_re, s5_c_im, s5_d, w_glu, b_glu):
    bp, lp, d = x_prompt.shape
    bs, ls, _ = x_sample.shape
    depth = norm_g.shape[0]
    e = conv_w.shape[-1]
    mem_len = mem_prompt.shape[1]
    n_grp, n_state = s5_lam_re.shape[1:]
    gd = e // SGU_GROUPS
    n_blk = e // S5_FEATS_PER_BLOCK
    kinds = [i % N_MIXERS for i in range(depth)]
    kidx = [sum(1 for j in range(i) if kinds[j] == kinds[i]) for i in range(depth)]

    mk, mv, mkb, mvb = mem_kv(mem_prompt.reshape(bp * mem_len, d), mem_norm_g, w_kv.astype(BF16))
    mem_k_prompt = mk.reshape(depth, bp, mem_len, XA_HEADS, XA_HEAD_DIM)
    mem_v_prompt = mv.reshape(depth, bp, mem_len, XA_HEADS, XA_HEAD_DIM)
    mkb = mkb.reshape(depth, bp, mem_len, XA_DIM)
    mvb = mvb.reshape(depth, bp, mem_len, XA_DIM)
    ck = cache_mem_k.reshape(depth, bs, mem_len, XA_DIM)
    cv = cache_mem_v.reshape(depth, bs, mem_len, XA_DIM)

    xp = x_prompt.reshape(bp * lp, d)
    xs = x_sample.transpose(1, 0, 2).reshape(ls * bs, d)
    q_pad = V7X_SUBLANES - ls

    def sample_attention(proj, qcol, i):
        q = proj.reshape(ls, bs, -1)[:, :, qcol * XA_DIM:(qcol + 1) * XA_DIM].transpose(1, 0, 2)
        q = jnp.pad(q, ((0, 0), (0, q_pad), (0, 0)))
        xa = cross_attention(q, 0, ck[i], cv[i], nb=8, tq=V7X_SUBLANES, out_dtype=F32)
        return xa[:, :ls].transpose(1, 0, 2).reshape(ls * bs, XA_DIM).astype(BF16)

    conv_p, conv_s, s5r_p, s5i_p, s5r_s, s5i_s, chunk_v = [], [], [], [], [], [], []
    for i in range(depth):
        kind, j = kinds[i], kidx[i]
        w1 = w_out[i, :e].astype(BF16)
        w2 = w_out[i, e:].astype(BF16)
        last = i == depth - 1
        if kind == 0:
            w = w_in_a[j].astype(BF16)
            qcol = 3 * e // XA_DIM
            pp = norm_proj(xp, norm_g[i], w, tm=1024)
            ps = norm_proj(xs, norm_g[i], w, tm=512)
            bias = jnp.repeat(sgu_b[j].T, gd, axis=1)
            yp = sgu_prompt(pp, sgu_norm_g[j], sgu_w[j], bias, e=e)
            coef = jnp.repeat(sgu_w[j][:, :ls, :ls].transpose(1, 2, 0).reshape(ls * ls, SGU_GROUPS), gd, axis=1)
            ys, vn = sgu_sample(ps.reshape(ls, bs, -1), sgu_norm_g[j], coef, bias[:V7X_SUBLANES], e=e)
            chunk_v.append(vn.transpose(1, 0, 2))
            ys = ys.reshape(ls * bs, e)
            ap = cross_attention(pp.reshape(bp, lp, -1), qcol, mkb[i], mvb[i], nb=1, tq=512, out_dtype=BF16)
            ap = ap.reshape(bp * lp, XA_DIM)
            as_ = sample_attention(ps, qcol, i)
        elif kind == 1:
            w = w_in_b[j].astype(BF16)
            qcol = 4 * e // XA_DIM
            pp = norm_proj(xp, norm_g[i], w, tm=1024)
            ps = norm_proj(xs, norm_g[i], w, tm=512)
            yp, cst = conv_prompt(pp, conv_w[j], jnp.zeros((bp, CONV_W - 1, e), F32), e=e, b=bp)
            conv_p.append(cst)
            ys, nst = conv_sample(ps.reshape(ls, bs, -1), conv_w[j], state_conv[j].transpose(1, 0, 2), e=e)
            conv_s.append(nst.transpose(1, 0, 2))
            ys = ys.reshape(ls * bs, e)
            ap = cross_attention(pp.reshape(bp, lp, -1), qcol, mkb[i], mvb[i], nb=1, tq=512, out_dtype=BF16)
            ap = ap.reshape(bp * lp, XA_DIM)
            as_ = sample_attention(ps, qcol, i)
        else:
            wc = w_in_c[j].astype(BF16)
            w_uz, w_q = wc[:, :2 * e], wc[:, 2 * e:]
            ar, ai, bbr, bbi = s5_discretize(s5_lam_re[j], s5_lam_im[j], s5_log_dt[j],
                                             s5_b_re[j].transpose(0, 2, 1), s5_b_im[j].transpose(0, 2, 1))
            bre = _block_diag(bbr, n_blk).astype(BF16)
            bim = _block_diag(bbi, n_blk).astype(BF16)
            cre = _block_diag(s5_c_re[j].transpose(0, 2, 1), n_blk).astype(BF16)
            cim = _block_diag(s5_c_im[j].transpose(0, 2, 1), n_blk).astype(BF16)
            ar, ai = ar.reshape(-1), ai.reshape(-1)

            xt = xp.reshape(bp, lp, d).transpose(1, 0, 2).reshape(lp * bp, d)
            pt = norm_proj(xt, norm_g[i], w_uz, tm=1024)
            pq = norm_proj(xp, norm_g[i], w_q, tm=1024)
            zero = jnp.zeros((bp, n_grp * n_state), F32)
            gp, hr, hi = s5_scan(pt.reshape(lp, bp, -1), bre, bim, cre, cim, ar, ai, s5_d[j], zero, zero,
                                 e=e, tl=32, nbb=bp)
            s5r_p.append(hr.reshape(bp, n_grp, n_state))
            s5i_p.append(hi.reshape(bp, n_grp, n_state))
            yt = glu_gate(gp.reshape(lp * bp, e), w_glu[j].astype(BF16), b_glu[j], pt, e=e)
            ap = cross_attention(pq.reshape(bp, lp, -1), 0, mkb[i], mvb[i], nb=1, tq=512, out_dtype=BF16)
            at = ap.transpose(1, 0, 2).reshape(lp * bp, XA_DIM)
            xt = out_proj(yt, at, w1, w2, xt, final_g, final_norm=last)
            xp = xt.reshape(lp, bp, d).transpose(1, 0, 2).reshape(bp * lp, d)

            ps = norm_proj(xs, norm_g[i], wc, tm=512)
            gs, hr, hi = s5_scan(ps.reshape(ls, bs, -1), bre, bim, cre, cim, ar, ai, s5_d[j],
                                 state_s5_re[j].reshape(bs, -1), state_s5_im[j].reshape(bs, -1),
                                 e=e, tl=ls, nbb=32)
            s5r_s.append(hr.reshape(bs, n_grp, n_state))
            s5i_s.append(hi.reshape(bs, n_grp, n_state))
            ys = glu_gate(gs.reshape(ls * bs, e), w_glu[j].astype(BF16), b_glu[j], ps, e=e)
            as_ = sample_attention(ps, 2 * e // XA_DIM, i)
        if kind != 2:
            xp = out_proj(yp, ap, w1, w2, xp, final_g, final_norm=last)
        xs = out_proj(ys, as_, w1, w2, xs, final_g, final_norm=last)

    y_prompt = xp.reshape(bp, lp, d)
    y_sample = xs.reshape(ls, bs, d).transpose(1, 0, 2)
    return (y_prompt, y_sample, mem_k_prompt, mem_v_prompt, jnp.stack(conv_p), jnp.stack(conv_s),
            jnp.stack(s5r_p), jnp.stack(s5i_p), jnp.stack(s5r_s), jnp.stack(s5i_s), jnp.stack(chunk_v))
```

```python
import functools

import jax
import jax.numpy as jnp
from jax import lax
from jax.experimental import pallas as pl
from jax.experimental.pallas import tpu as pltpu

F32 = jnp.float32
BF16 = jnp.bfloat16

EPS = 1e-6
N_MIXERS = 3
CHUNK = 128
SGU_GROUPS = 8
CONV_W = 3
S5_IN = 16
XA_HEADS = 4
XA_HEAD_DIM = 128
XA_DIM = XA_HEADS * XA_HEAD_DIM

V7X_SUBLANES = 8
V7X_VMEM_LIMIT_BYTES = 56 * 1024 * 1024
S5_FEATS_PER_BLOCK = 256
S5_SCAN_LANES = 512


def _params(*semantics):
    return pltpu.CompilerParams(dimension_semantics=semantics, vmem_limit_bytes=V7X_VMEM_LIMIT_BYTES)


def _resident(shape):
    nd = len(shape)
    return pl.BlockSpec(shape, lambda *_: (0,) * nd, pipeline_mode=pl.Buffered(1))


def _rms(x, g):
    return x * lax.rsqrt(jnp.mean(x * x, axis=-1, keepdims=True) + EPS) * g


def _silu(z):
    return z * jax.nn.sigmoid(z)


def _norm_proj_kernel(x_ref, g_ref, w_ref, o_ref, hn_ref):
    @pl.when(pl.program_id(1) == 0)
    def _():
        hn_ref[...] = _rms(x_ref[...], g_ref[...]).astype(BF16)

    o_ref[...] = jnp.dot(hn_ref[...], w_ref[...], preferred_element_type=F32)


def norm_proj(x, g, w, *, tm, tn=512):
    m, d = x.shape
    n = w.shape[1]
    tm = min(tm, m)
    return pl.pallas_call(
        _norm_proj_kernel,
        grid=(m // tm, n // tn),
        in_specs=[pl.BlockSpec((tm, d), lambda i, j: (i, 0)),
                  pl.BlockSpec((1, d), lambda i, j: (0, 0)),
                  pl.BlockSpec((d, tn), lambda i, j: (0, j))],
        out_specs=pl.BlockSpec((tm, tn), lambda i, j: (i, j)),
        out_shape=jax.ShapeDtypeStruct((m, n), F32),
        scratch_shapes=[pltpu.VMEM((tm, d), BF16)],
        compiler_params=_params("parallel", "arbitrary"),
        name="norm_proj",
    )(x, g.reshape(1, d), w)


def _mem_kv_kernel(x_ref, g_ref, w_ref, k_ref, v_ref, kb_ref, vb_ref):
    hn = _rms(x_ref[...], g_ref[0]).astype(BF16)
    kv = jnp.dot(hn, w_ref[0], preferred_element_type=F32)
    k, v = kv[:, :XA_DIM], kv[:, XA_DIM:]
    k_ref[0] = k
    v_ref[0] = v
    kb_ref[0] = k.astype(BF16)
    vb_ref[0] = v.astype(BF16)


def mem_kv(mem, g, w, *, tm=512):
    m, d = mem.shape
    depth = g.shape[0]
    out = pl.BlockSpec((1, tm, XA_DIM), lambda l, i: (l, i, 0))
    shape = lambda dt: jax.ShapeDtypeStruct((depth, m, XA_DIM), dt)
    return pl.pallas_call(
        _mem_kv_kernel,
        grid=(depth, m // tm),
        in_specs=[pl.BlockSpec((tm, d), lambda l, i: (i, 0)),
                  pl.BlockSpec((1, 1, d), lambda l, i: (l, 0, 0)),
                  pl.BlockSpec((1, d, 2 * XA_DIM), lambda l, i: (l, 0, 0))],
        out_specs=[out, out, out, out],
        out_shape=[shape(F32), shape(F32), shape(BF16), shape(BF16)],
        compiler_params=_params("parallel", "parallel"),
        name="mem_kv",
    )(mem, g.reshape(depth, 1, d), w)


def _attn_kernel(q_ref, k_ref, v_ref, o_ref, *, nb):
    scale = XA_HEAD_DIM ** -0.5
    for b in range(nb):
        q = q_ref[b].astype(BF16)
        k = k_ref[b].astype(BF16)
        v = v_ref[b].astype(BF16)
        for h in range(XA_HEADS):
            hs = slice(h * XA_HEAD_DIM, (h + 1) * XA_HEAD_DIM)
            s = lax.dot_general(q[:, hs], k[:, hs], (((1,), (1,)), ((), ())),
                                preferred_element_type=F32) * scale
            e = jnp.exp(s - jnp.max(s, axis=-1, keepdims=True))
            p = e / jnp.sum(e, axis=-1, keepdims=True)
            o = jnp.dot(p.astype(BF16), v[:, hs], preferred_element_type=F32)
            o_ref[b, :, hs] = o.astype(o_ref.dtype)


def cross_attention(q, qcol, k, v, *, nb, tq, out_dtype):
    b, l, _ = q.shape
    mem = k.shape[1]
    return pl.pallas_call(
        functools.partial(_attn_kernel, nb=nb),
        grid=(b // nb, l // tq),
        in_specs=[pl.BlockSpec((nb, tq, XA_DIM), lambda i, j: (i, j, qcol)),
                  pl.BlockSpec((nb, mem, XA_DIM), lambda i, j: (i, 0, 0)),
                  pl.BlockSpec((nb, mem, XA_DIM), lambda i, j: (i, 0, 0))],
        out_specs=pl.BlockSpec((nb, tq, XA_DIM), lambda i, j: (i, j, 0)),
        out_shape=jax.ShapeDtypeStruct((b, l, XA_DIM), out_dtype),
        compiler_params=_params("parallel", "arbitrary"),
        name="cross_attention",
    )(q, k, v)


def _attn_sample_kernel(q_ref, k_ref, v_ref, bias_ref, o_ref, *, nb):
    scale = XA_HEAD_DIM ** -0.5
    for b in range(nb):
        q = q_ref[b].astype(BF16)
        k = k_ref[b].astype(BF16)
        v = v_ref[b].astype(BF16)
        s = lax.dot_general(q, k, (((1,), (1,)), ((), ())), preferred_element_type=F32) * scale + bias_ref[...]
        e = jnp.exp(s - jnp.max(s, axis=-1, keepdims=True))
        p = e / jnp.sum(e, axis=-1, keepdims=True)
        o_ref[b] = jnp.dot(p.astype(BF16), v, preferred_element_type=F32)


def cross_attention_sample(q, cache_k, cache_v, bias, layer, *, nb=8):
    b, nq, dh = q.shape
    nkv = cache_k.shape[2]
    kv = pl.BlockSpec((None, nb, nkv, dh), lambda i: (layer, i, 0, 0))
    return pl.pallas_call(
        functools.partial(_attn_sample_kernel, nb=nb),
        grid=(b // nb,),
        in_specs=[pl.BlockSpec((nb, nq, dh), lambda i: (i, 0, 0)), kv, kv, _resident((nq, nkv))],
        out_specs=pl.BlockSpec((nb, nq, dh), lambda i: (i, 0, 0)),
        out_shape=jax.ShapeDtypeStruct((b, nq, dh), F32),
        compiler_params=_params("parallel"),
        name="cross_attention_sample",
    )(q, cache_k, cache_v, bias)


def _out_proj_kernel(y_ref, xa_ref, w1_ref, w2_ref, x_ref, fg_ref, o_ref, *, final_norm):
    acc = x_ref[...] + jnp.dot(y_ref[...], w1_ref[...], preferred_element_type=F32)
    acc = acc + jnp.dot(xa_ref[...], w2_ref[...], preferred_element_type=F32)
    o_ref[...] = _rms(acc, fg_ref[...]) if final_norm else acc


def out_proj(y, xa, w1, w2, x, final_g, *, final_norm, tm=512):
    m, d = x.shape
    e, a = y.shape[1], xa.shape[1]
    tm = min(tm, m)
    return pl.pallas_call(
        functools.partial(_out_proj_kernel, final_norm=final_norm),
        grid=(m // tm,),
        in_specs=[pl.BlockSpec((tm, e), lambda i: (i, 0)),
                  pl.BlockSpec((tm, a), lambda i: (i, 0)),
                  _resident((e, d)),
                  _resident((a, d)),
                  pl.BlockSpec((tm, d), lambda i: (i, 0)),
                  _resident((1, d))],
        out_specs=pl.BlockSpec((tm, d), lambda i: (i, 0)),
        out_shape=jax.ShapeDtypeStruct((m, d), F32),
        compiler_params=_params("parallel"),
        name="out_proj",
    )(y, xa, w1, w2, x, final_g.reshape(1, d))


def _sgu_prompt_kernel(u_ref, v_ref, z_ref, g_ref, ws_ref, bias_ref, o_ref, *, chunks):
    gd = u_ref.shape[1] // SGU_GROUPS
    row = lax.broadcasted_iota(jnp.int32, (CHUNK, CHUNK), 0)
    col = lax.broadcasted_iota(jnp.int32, (CHUNK, CHUNK), 1)
    ws = [jnp.where(row >= col, ws_ref[g], 0.0).astype(BF16) for g in range(SGU_GROUPS)]
    for c in range(chunks):
        rs = slice(c * CHUNK, (c + 1) * CHUNK)
        vn = _rms(v_ref[rs, :], g_ref[...]).astype(BF16)
        for g in range(SGU_GROUPS):
            cs = slice(g * gd, (g + 1) * gd)
            mixed = jnp.dot(ws[g], vn[:, cs], preferred_element_type=F32) + bias_ref[:, cs]
            o_ref[rs, cs] = (u_ref[rs, cs] * mixed * _silu(z_ref[rs, cs])).astype(BF16)


def sgu_prompt(proj, g, ws, bias, *, e, chunks=2):
    m = proj.shape[0]
    tm = chunks * CHUNK
    col = lambda c: pl.BlockSpec((tm, e), lambda i: (i, c))
    return pl.pallas_call(
        functools.partial(_sgu_prompt_kernel, chunks=chunks),
        grid=(m // tm,),
        in_specs=[col(0), col(1), col(2), _resident((1, e)), _resident(ws.shape), _resident(bias.shape)],
        out_specs=pl.BlockSpec((tm, e), lambda i: (i, 0)),
        out_shape=jax.ShapeDtypeStruct((m, e), BF16),
        compiler_params=_params("parallel"),
        name="sgu_prompt",
    )(proj, proj, proj, g.reshape(1, e), ws, bias)


def _sgu_sample_kernel(u_ref, v_ref, z_ref, g_ref, coef_ref, bias_ref, o_ref, vn_ref):
    steps = u_ref.shape[0]
    vn = []
    for t in range(steps):
        n = _rms(v_ref[t], g_ref[...])
        vn_ref[t] = n
        vn.append(n)
    for t in range(steps):
        mixed = bias_ref[t:t + 1, :]
        for s in range(t + 1):
            mixed = mixed + coef_ref[t * steps + s:t * steps + s + 1, :] * vn[s]
        o_ref[t] = (u_ref[t] * mixed * _silu(z_ref[t])).astype(BF16)


def sgu_sample(proj, g, coef, bias, *, e, nbb=32):
    t, b, _ = proj.shape
    col = lambda c: pl.BlockSpec((t, nbb, e), lambda i: (0, i, c))
    out = pl.BlockSpec((t, nbb, e), lambda i: (0, i, 0))
    return pl.pallas_call(
        _sgu_sample_kernel,
        grid=(b // nbb,),
        in_specs=[col(0), col(1), col(2), _resident((1, e)), _resident(coef.shape), _resident(bias.shape)],
        out_specs=[out, out],
        out_shape=[jax.ShapeDtypeStruct((t, b, e), BF16), jax.ShapeDtypeStruct((t, b, e), F32)],
        compiler_params=_params("parallel"),
        name="sgu_sample",
    )(proj, proj, proj, g.reshape(1, e), coef, bias)


CONV_EDGE_ROWS = 16


def _conv_prompt_kernel(bg_ref, cg_ref, hv_ref, z_ref, w_ref, init_ref, o_ref, st_ref, carry_ref):
    i = pl.program_id(1)
    tm = cg_ref.shape[0]
    edge = CONV_EDGE_ROWS

    @pl.when(i == 0)
    def _():
        carry_ref[...] = jnp.zeros_like(carry_ref)
        carry_ref[edge - (CONV_W - 1):edge, :] = init_ref[0]

    w0, w1, w2 = w_ref[0:1, :], w_ref[1:2, :], w_ref[2:3, :]
    xc = cg_ref[...] * hv_ref[...]
    r1 = pltpu.roll(xc, 1, 0)
    r2 = pltpu.roll(xc, 2, 0)
    y = w0 * r2 + w1 * r1 + w2 * xc
    o_ref[...] = (bg_ref[...] * y * _silu(z_ref[...])).astype(BF16)

    prev = carry_ref[...]
    rid = lax.broadcasted_iota(jnp.int32, (edge, xc.shape[1]), 0)
    x1 = jnp.where(rid < 1, pltpu.roll(prev, 1, 0), r1[:edge])
    x2 = jnp.where(rid < 2, pltpu.roll(prev, 2, 0), r2[:edge])
    ye = w0 * x2 + w1 * x1 + w2 * xc[:edge]
    o_ref[0:edge, :] = (bg_ref[0:edge, :] * ye * _silu(z_ref[0:edge, :])).astype(BF16)

    carry_ref[...] = xc[tm - edge:, :]
    st_ref[0] = xc[tm - (CONV_W - 1):, :]


def conv_prompt(proj, w, init, *, e, b, tm=256):
    m = proj.shape[0]
    lt = m // b // tm
    col = lambda c: pl.BlockSpec((tm, e), lambda bi, i: (bi * lt + i, c))
    return pl.pallas_call(
        _conv_prompt_kernel,
        grid=(b, lt),
        in_specs=[col(0), col(1), col(2), col(3), _resident((CONV_W, e)),
                  pl.BlockSpec((1, CONV_W - 1, e), lambda bi, i: (bi, 0, 0))],
        out_specs=[pl.BlockSpec((tm, e), lambda bi, i: (bi * lt + i, 0)),
                   pl.BlockSpec((1, CONV_W - 1, e), lambda bi, i: (bi, 0, 0))],
        out_shape=[jax.ShapeDtypeStruct((m, e), BF16), jax.ShapeDtypeStruct((b, CONV_W - 1, e), F32)],
        scratch_shapes=[pltpu.VMEM((CONV_EDGE_ROWS, e), F32)],
        compiler_params=_params("parallel", "arbitrary"),
        name="conv_prompt",
    )(proj, proj, proj, proj, w, init)


def _conv_sample_kernel(bg_ref, cg_ref, hv_ref, z_ref, w_ref, st_ref, o_ref, ns_ref):
    steps = cg_ref.shape[0]
    xp = [st_ref[k] for k in range(CONV_W - 1)] + [cg_ref[t] * hv_ref[t] for t in range(steps)]
    for t in range(steps):
        y = w_ref[0:1, :] * xp[t]
        for k in range(1, CONV_W):
            y = y + w_ref[k:k + 1, :] * xp[t + k]
        o_ref[t] = (bg_ref[t] * y * _silu(z_ref[t])).astype(BF16)
    for k in range(CONV_W - 1):
        ns_ref[k] = xp[steps + k]


def conv_sample(proj, w, state, *, e, nbb=32):
    t, b, _ = proj.shape
    col = lambda c: pl.BlockSpec((t, nbb, e), lambda i: (0, i, c))
    st = pl.BlockSpec((CONV_W - 1, nbb, e), lambda i: (0, i, 0))
    return pl.pallas_call(
        _conv_sample_kernel,
        grid=(b // nbb,),
        in_specs=[col(0), col(1), col(2), col(3), _resident((CONV_W, e)), st],
        out_specs=[pl.BlockSpec((t, nbb, e), lambda i: (0, i, 0)), st],
        out_shape=[jax.ShapeDtypeStruct((t, b, e), BF16), jax.ShapeDtypeStruct((CONV_W - 1, b, e), F32)],
        compiler_params=_params("parallel"),
        name="conv_sample",
    )(proj, proj, proj, proj, w, state)


def _s5_disc_kernel(lre_ref, lim_ref, ldt_ref, bre_ref, bim_ref, ar_ref, ai_ref, bbr_ref, bbi_ref):
    dt = jnp.exp(ldt_ref[...])
    lr, li = lre_ref[...], lim_ref[...]
    mag = jnp.exp(lr * dt)
    ar = mag * jnp.cos(li * dt)
    ai = mag * jnp.sin(li * dt)
    nr, ni = ar - 1, ai
    den = lr * lr + li * li
    cr = (nr * lr + ni * li) / den
    ci = (ni * lr - nr * li) / den
    ar_ref[...] = ar
    ai_ref[...] = ai
    cr3, ci3 = cr[:, None, :], ci[:, None, :]
    bbr_ref[...] = cr3 * bre_ref[...] - ci3 * bim_ref[...]
    bbi_ref[...] = cr3 * bim_ref[...] + ci3 * bre_ref[...]


def s5_discretize(lam_re, lam_im, log_dt, b_re, b_im):
    g, p = lam_re.shape
    gp = jax.ShapeDtypeStruct((g, p), F32)
    gip = jax.ShapeDtypeStruct(b_re.shape, F32)
    return pl.pallas_call(
        _s5_disc_kernel,
        out_shape=[gp, gp, gip, gip],
        compiler_params=pltpu.CompilerParams(vmem_limit_bytes=V7X_VMEM_LIMIT_BYTES),
        name="s5_discretize",
    )(lam_re, lam_im, log_dt.reshape(g, 1), b_re, b_im)


def _s5_kernel(u_ref, bre_ref, bim_ref, cre_ref, cim_ref, ar_ref, ai_ref, d_ref, h0r_ref, h0i_ref,
               g_ref, hr_ref, hi_ref, sr_ref, si_ref):
    tl, nb, e = u_ref.shape
    rows = tl * nb
    n_blk = bre_ref.shape[0]
    fk = bre_ref.shape[1]
    sk = bre_ref.shape[2]
    lanes = sr_ref.shape[1]

    @pl.when(pl.program_id(1) == 0)
    def _():
        hr_ref[...] = h0r_ref[...]
        hi_ref[...] = h0i_ref[...]

    u = u_ref[...].reshape(rows, e)
    ub = u.astype(BF16)
    for kb in range(n_blk):
        uk = ub[:, kb * fk:(kb + 1) * fk]
        sr_ref[:, kb * sk:(kb + 1) * sk] = jnp.dot(uk, bre_ref[kb], preferred_element_type=F32)
        si_ref[:, kb * sk:(kb + 1) * sk] = jnp.dot(uk, bim_ref[kb], preferred_element_type=F32)

    for c in range(lanes // S5_SCAN_LANES):
        ls = slice(c * S5_SCAN_LANES, (c + 1) * S5_SCAN_LANES)
        a_r = jnp.broadcast_to(ar_ref[:, ls], (V7X_SUBLANES, S5_SCAN_LANES))
        a_i = jnp.broadcast_to(ai_ref[:, ls], (V7X_SUBLANES, S5_SCAN_LANES))

        def sweep(sg, _):
            r0 = pl.multiple_of(sg * V7X_SUBLANES, V7X_SUBLANES)

            def step(t, h):
                hr, hi = h
                row = pl.multiple_of(t * nb + r0, V7X_SUBLANES)
                nr = a_r * hr - a_i * hi + sr_ref[pl.ds(row, V7X_SUBLANES), ls]
                ni = a_r * hi + a_i * hr + si_ref[pl.ds(row, V7X_SUBLANES), ls]
                sr_ref[pl.ds(row, V7X_SUBLANES), ls] = nr
                si_ref[pl.ds(row, V7X_SUBLANES), ls] = ni
                return nr, ni

            h0 = (hr_ref[pl.ds(r0, V7X_SUBLANES), ls], hi_ref[pl.ds(r0, V7X_SUBLANES), ls])
            hr, hi = lax.fori_loop(0, tl, step, h0, unroll=min(tl, 8))
            hr_ref[pl.ds(r0, V7X_SUBLANES), ls] = hr
            hi_ref[pl.ds(r0, V7X_SUBLANES), ls] = hi
            return 0

        lax.fori_loop(0, nb // V7X_SUBLANES, sweep, 0)

    for kb in range(n_blk):
        hrb = sr_ref[:, kb * sk:(kb + 1) * sk].astype(BF16)
        hib = si_ref[:, kb * sk:(kb + 1) * sk].astype(BF16)
        y = jnp.dot(hrb, cre_ref[kb], preferred_element_type=F32)
        y = y - jnp.dot(hib, cim_ref[kb], preferred_element_type=F32)
        fs = slice(kb * fk, (kb + 1) * fk)
        y = y + d_ref[:, fs] * u[:, fs]
        g_ref[:, :, fs] = jax.nn.gelu(y).reshape(tl, nb, fk)


def s5_scan(proj, bre, bim, cre, cim, ar, ai, d, h0r, h0i, *, e, tl, nbb):
    l, b, _ = proj.shape
    lanes = h0r.shape[1]
    rows = tl * nbb
    st = pl.BlockSpec((nbb, lanes), lambda bi, i: (bi, 0))
    return pl.pallas_call(
        _s5_kernel,
        grid=(b // nbb, l // tl),
        in_specs=[pl.BlockSpec((tl, nbb, e), lambda bi, i: (i, bi, 0)),
                  _resident(bre.shape), _resident(bim.shape), _resident(cre.shape), _resident(cim.shape),
                  _resident((1, lanes)), _resident((1, lanes)), _resident((1, e)), st, st],
        out_specs=[pl.BlockSpec((tl, nbb, e), lambda bi, i: (i, bi, 0)), st, st],
        out_shape=[jax.ShapeDtypeStruct((l, b, e), F32),
                   jax.ShapeDtypeStruct((b, lanes), F32), jax.ShapeDtypeStruct((b, lanes), F32)],
        scratch_shapes=[pltpu.VMEM((rows, lanes), F32), pltpu.VMEM((rows, lanes), F32)],
        compiler_params=_params("parallel", "arbitrary"),
        name="s5_scan",
    )(proj, bre, bim, cre, cim, ar.reshape(1, lanes), ai.reshape(1, lanes), d.reshape(1, e), h0r, h0i)


def _glu_kernel(g_ref, w_ref, b_ref, gj_ref, z_ref, o_ref, gb_ref):
    @pl.when(pl.program_id(1) == 0)
    def _():
        gb_ref[...] = g_ref[...].astype(BF16)

    a = jnp.dot(gb_ref[...], w_ref[...], preferred_element_type=F32) + b_ref[...]
    o_ref[...] = (gj_ref[...] * jax.nn.sigmoid(a) * _silu(z_ref[...])).astype(BF16)


def glu_gate(g, w, bias, proj, *, e, tm=512, tn=512):
    m = g.shape[0]
    tm = min(tm, m)
    return pl.pallas_call(
        _glu_kernel,
        grid=(m // tm, e // tn),
        in_specs=[pl.BlockSpec((tm, e), lambda i, j: (i, 0)),
                  pl.BlockSpec((e, tn), lambda i, j: (0, j)),
                  pl.BlockSpec((1, tn), lambda i, j: (0, j)),
                  pl.BlockSpec((tm, tn), lambda i, j: (i, j)),
                  pl.BlockSpec((tm, tn), lambda i, j: (i, e // tn + j))],
        out_specs=pl.BlockSpec((tm, tn), lambda i, j: (i, j)),
        out_shape=jax.ShapeDtypeStruct((m, e), BF16),
        scratch_shapes=[pltpu.VMEM((tm, e), BF16)],
        compiler_params=_params("parallel", "arbitrary"),
        name="glu_gate",
    )(g, w, bias.reshape(1, e), g, proj)


def _block_diag(x, n_blk):
    g, r, c = x.shape
    gb = g // n_blk
    eye = jnp.eye(gb, dtype=x.dtype)
    out = jnp.einsum("bgrc,gh->bgrhc", x.reshape(n_blk, gb, r, c), eye)
    return out.reshape(n_blk, gb * r, gb * c)


def kernel(x_prompt, x_sample, mem_prompt, cache_mem_k, cache_mem_v, state_conv, state_s5_re, state_s5_im,
           norm_g, final_g, mem_norm_g, w_kv, w_out, w_in_a, sgu_norm_g, sgu_w, sgu_b, w_in_b, conv_w, w_in_c,
           s5_lam_re, s5_lam_im, s5_log_dt, s5_b_re, s5_b_im, s5_c_re, s5_c_im, s5_d, w_glu, b_glu):
    bp, lp, d = x_prompt.shape
    bs, ls, _ = x_sample.shape
    depth = norm_g.shape[0]
    e = conv_w.shape[-1]
    mem_len = mem_prompt.shape[1]
    n_grp, n_state = s5_lam_re.shape[1:]
    gd = e // SGU_GROUPS
    n_blk = e // S5_FEATS_PER_BLOCK
    kinds = [i % N_MIXERS for i in range(depth)]
    kidx = [sum(1 for j in range(i) if kinds[j] == kinds[i]) for i in range(depth)]

    mk, mv, mkb, mvb = mem_kv(mem_prompt.reshape(bp * mem_len, d), mem_norm_g, w_kv.astype(BF16))
    mem_k_prompt = mk.reshape(depth, bp, mem_len, XA_HEADS, XA_HEAD_DIM)
    mem_v_prompt = mv.reshape(depth, bp, mem_len, XA_HEADS, XA_HEAD_DIM)
    mkb = mkb.reshape(depth, bp, mem_len, XA_DIM)
    mvb = mvb.reshape(depth, bp, mem_len, XA_DIM)
    ck = cache_mem_k.reshape(depth, bs, mem_len * XA_HEADS, XA_HEAD_DIM)
    cv = cache_mem_v.reshape(depth, bs, mem_len * XA_HEADS, XA_HEAD_DIM)
    q_head = jnp.arange(XA_HEADS * ls, dtype=jnp.int32)[:, None] // ls
    kv_head = jnp.arange(mem_len * XA_HEADS, dtype=jnp.int32)[None, :] % XA_HEADS
    head_bias = jnp.where(q_head == kv_head, 0.0, -1e30).astype(F32)

    xp = x_prompt.reshape(bp * lp, d)
    xs = x_sample.transpose(1, 0, 2).reshape(ls * bs, d)

    def sample_attention(proj, qcol, i):
        q = proj.reshape(ls, bs, -1)[:, :, qcol * XA_DIM:(qcol + 1) * XA_DIM]
        q = q.reshape(ls, bs, XA_HEADS, XA_HEAD_DIM).transpose(1, 2, 0, 3).reshape(bs, XA_HEADS * ls, XA_HEAD_DIM)
        xa = cross_attention_sample(q, ck, cv, head_bias, i)
        xa = xa.reshape(bs, XA_HEADS, ls, XA_HEAD_DIM).transpose(2, 0, 1, 3)
        return xa.reshape(ls * bs, XA_DIM).astype(BF16)

    conv_p, conv_s, s5r_p, s5i_p, s5r_s, s5i_s, chunk_v = [], [], [], [], [], [], []
    for i in range(depth):
        kind, j = kinds[i], kidx[i]
        w1 = w_out[i, :e].astype(BF16)
        w2 = w_out[i, e:].astype(BF16)
        last = i == depth - 1
        if kind == 0:
            w = w_in_a[j].astype(BF16)
            qcol = 3 * e // XA_DIM
            pp = norm_proj(xp, norm_g[i], w, tm=2048)
            ps = norm_proj(xs, norm_g[i], w, tm=512)
            bias = jnp.repeat(sgu_b[j].T, gd, axis=1)
            yp = sgu_prompt(pp, sgu_norm_g[j], sgu_w[j], bias, e=e)
            coef = jnp.repeat(sgu_w[j][:, :ls, :ls].transpose(1, 2, 0).reshape(ls * ls, SGU_GROUPS), gd, axis=1)
            ys, vn = sgu_sample(ps.reshape(ls, bs, -1), sgu_norm_g[j], coef, bias[:V7X_SUBLANES], e=e)
            chunk_v.append(vn.transpose(1, 0, 2))
            ys = ys.reshape(ls * bs, e)
            ap = cross_attention(pp.reshape(bp, lp, -1), qcol, mkb[i], mvb[i], nb=1, tq=512, out_dtype=BF16)
            ap = ap.reshape(bp * lp, XA_DIM)
            as_ = sample_attention(ps, qcol, i)
        elif kind == 1:
            w = w_in_b[j].astype(BF16)
            qcol = 4 * e // XA_DIM
            pp = norm_proj(xp, norm_g[i], w, tm=2048)
            ps = norm_proj(xs, norm_g[i], w, tm=512)
            yp, cst = conv_prompt(pp, conv_w[j], jnp.zeros((bp, CONV_W - 1, e), F32), e=e, b=bp)
            conv_p.append(cst)
            ys, nst = conv_sample(ps.reshape(ls, bs, -1), conv_w[j], state_conv[j].transpose(1, 0, 2), e=e)
            conv_s.append(nst.transpose(1, 0, 2))
            ys = ys.reshape(ls * bs, e)
            ap = cross_attention(pp.reshape(bp, lp, -1), qcol, mkb[i], mvb[i], nb=1, tq=512, out_dtype=BF16)
            ap = ap.reshape(bp * lp, XA_DIM)
            as_ = sample_attention(ps, qcol, i)
        else:
            wc = w_in_c[j].astype(BF16)
            w_uz, w_q = wc[:, :2 * e], wc[:, 2 * e:]
            ar, ai, bbr, bbi = s5_discretize(s5_lam_re[j], s5_lam_im[j], s5_log_dt[j],
                                             s5_b_re[j].transpose(0, 2, 1), s5_b_im[j].transpose(0, 2, 1))
            bre = _block_diag(bbr, n_blk).astype(BF16)
            bim = _block_diag(bbi, n_blk).astype(BF16)
            cre = _block_diag(s5_c_re[j].transpose(0, 2, 1), n_blk).astype(BF16)
            cim = _block_diag(s5_c_im[j].transpose(0, 2, 1), n_blk).astype(BF16)
            ar, ai = ar.reshape(-1), ai.reshape(-1)

            xt = xp.reshape(bp, lp, d).transpose(1, 0, 2).reshape(lp * bp, d)
            pt = norm_proj(xt, norm_g[i], w_uz, tm=2048)
            pq = norm_proj(xp, norm_g[i], w_q, tm=2048)
            zero = jnp.zeros((bp, n_grp * n_state), F32)
            gp, hr, hi = s5_scan(pt.reshape(lp, bp, -1), bre, bim, cre, cim, ar, ai, s5_d[j], zero, zero,
                                 e=e, tl=32, nbb=bp)
            s5r_p.append(hr.reshape(bp, n_grp, n_state))
            s5i_p.append(hi.reshape(bp, n_grp, n_state))
            yt = glu_gate(gp.reshape(lp * bp, e), w_glu[j].astype(BF16), b_glu[j], pt, e=e)
            ap = cross_attention(pq.reshape(bp, lp, -1), 0, mkb[i], mvb[i], nb=1, tq=512, out_dtype=BF16)
            at = ap.transpose(1, 0, 2).reshape(lp * bp, XA_DIM)
            xt = out_proj(yt, at, w1, w2, xt, final_g, final_norm=last)
            xp = xt.reshape(lp, bp, d).transpose(1, 0, 2).reshape(bp * lp, d)

            ps = norm_proj(xs, norm_g[i], wc, tm=512)
            gs, hr, hi = s5_scan(ps.reshape(ls, bs, -1), bre, bim, cre, cim, ar, ai, s5_d[j],
                                 state_s5_re[j].reshape(bs, -1), state_s5_im[j].reshape(bs, -1),
                                 e=e, tl=ls, nbb=32)
            s5r_s.append(hr.reshape(bs, n_grp, n_state))
            s5i_s.append(hi.reshape(bs, n_grp, n_state))
            ys = glu_gate(gs.reshape(ls * bs, e), w_glu[j].astype(BF16), b_glu[j], ps, e=e)
            as_ = sample_attention(ps, 2 * e // XA_DIM, i)
        if kind != 2:
            xp = out_proj(yp, ap, w1, w2, xp, final_g, final_norm=last)
        xs = out_proj(ys, as_, w1, w2, xs, final_g, final_norm=last)

    y_prompt = xp.reshape(bp, lp, d)
    y_sample = xs.reshape(ls, bs, d).transpose(1, 0, 2)
    return (y_prompt, y_sample, mem_k_prompt, mem_v_prompt, jnp.stack(conv_p), jnp.stack(conv_s),
            jnp.stack(s5r_p), jnp.stack(s5i_p), jnp.stack(s5r_s), jnp.stack(s5i_s), jnp.stack(chunk_v))
```

```python
import functools

import jax
import jax.numpy as jnp
from jax import lax
from jax.experimental import pallas as pl
from jax.experimental.pallas import tpu as pltpu

F32 = jnp.float32
BF16 = jnp.bfloat16

EPS = 1e-6
N_MIXERS = 3
CHUNK = 128
SGU_GROUPS = 8
CONV_W = 3
S5_IN = 16
XA_HEADS = 4
XA_HEAD_DIM = 128
XA_DIM = XA_HEADS * XA_HEAD_DIM

V7X_SUBLANES = 8
V7X_VMEM_LIMIT_BYTES = 56 * 1024 * 1024
S5_FEATS_PER_BLOCK = 256
S5_SCAN_LANES = 512


def _params(*semantics):
    return pltpu.CompilerParams(dimension_semantics=semantics, vmem_limit_bytes=V7X_VMEM_LIMIT_BYTES)


def _resident(shape):
    nd = len(shape)
    return pl.BlockSpec(shape, lambda *_: (0,) * nd, pipeline_mode=pl.Buffered(1))


def _rms(x, g):
    return x * lax.rsqrt(jnp.mean(x * x, axis=-1, keepdims=True) + EPS) * g


def _silu(z):
    return z * jax.nn.sigmoid(z)


def _norm_proj_kernel(x_ref, g_ref, w_ref, o_ref, hn_ref):
    @pl.when(pl.program_id(1) == 0)
    def _():
        hn_ref[...] = _rms(x_ref[...], g_ref[...]).astype(BF16)

    o_ref[...] = jnp.dot(hn_ref[...], w_ref[...], preferred_element_type=F32)


def norm_proj(x, g, w, *, tm, tn=512):
    m, d = x.shape
    n = w.shape[1]
    tm = min(tm, m)
    return pl.pallas_call(
        _norm_proj_kernel,
        grid=(m // tm, n // tn),
        in_specs=[pl.BlockSpec((tm, d), lambda i, j: (i, 0)),
                  pl.BlockSpec((1, d), lambda i, j: (0, 0)),
                  pl.BlockSpec((d, tn), lambda i, j: (0, j))],
        out_specs=pl.BlockSpec((tm, tn), lambda i, j: (i, j)),
        out_shape=jax.ShapeDtypeStruct((m, n), F32),
        scratch_shapes=[pltpu.VMEM((tm, d), BF16)],
        compiler_params=_params("parallel", "arbitrary"),
        name="norm_proj",
    )(x, g.reshape(1, d), w)


def _mem_kv_kernel(x_ref, g_ref, w_ref, k_ref, v_ref, kb_ref, vb_ref):
    hn = _rms(x_ref[...], g_ref[0]).astype(BF16)
    kv = jnp.dot(hn, w_ref[0], preferred_element_type=F32)
    k, v = kv[:, :XA_DIM], kv[:, XA_DIM:]
    k_ref[0] = k
    v_ref[0] = v
    kb_ref[0] = k.astype(BF16)
    vb_ref[0] = v.astype(BF16)


def mem_kv(mem, g, w, *, tm=512):
    m, d = mem.shape
    depth = g.shape[0]
    out = pl.BlockSpec((1, tm, XA_DIM), lambda l, i: (l, i, 0))
    shape = lambda dt: jax.ShapeDtypeStruct((depth, m, XA_DIM), dt)
    return pl.pallas_call(
        _mem_kv_kernel,
        grid=(depth, m // tm),
        in_specs=[pl.BlockSpec((tm, d), lambda l, i: (i, 0)),
                  pl.BlockSpec((1, 1, d), lambda l, i: (l, 0, 0)),
                  pl.BlockSpec((1, d, 2 * XA_DIM), lambda l, i: (l, 0, 0))],
        out_specs=[out, out, out, out],
        out_shape=[shape(F32), shape(F32), shape(BF16), shape(BF16)],
        compiler_params=_params("parallel", "parallel"),
        name="mem_kv",
    )(mem, g.reshape(depth, 1, d), w)


def _attn_kernel(q_ref, k_ref, v_ref, o_ref, *, nb):
    scale = XA_HEAD_DIM ** -0.5
    for b in range(nb):
        q = q_ref[b].astype(BF16)
        k = k_ref[b].astype(BF16)
        v = v_ref[b].astype(BF16)
        for h in range(XA_HEADS):
            hs = slice(h * XA_HEAD_DIM, (h + 1) * XA_HEAD_DIM)
            s = lax.dot_general(q[:, hs], k[:, hs], (((1,), (1,)), ((), ())),
                                preferred_element_type=F32) * scale
            e = jnp.exp(s - jnp.max(s, axis=-1, keepdims=True))
            p = e / jnp.sum(e, axis=-1, keepdims=True)
            o = jnp.dot(p.astype(BF16), v[:, hs], preferred_element_type=F32)
            o_ref[b, :, hs] = o.astype(o_ref.dtype)


def cross_attention(q, qcol, k, v, *, nb, tq, out_dtype):
    b, l, _ = q.shape
    mem = k.shape[1]
    return pl.pallas_call(
        functools.partial(_attn_kernel, nb=nb),
        grid=(b // nb, l // tq),
        in_specs=[pl.BlockSpec((nb, tq, XA_DIM), lambda i, j: (i, j, qcol)),
                  pl.BlockSpec((nb, mem, XA_DIM), lambda i, j: (i, 0, 0)),
                  pl.BlockSpec((nb, mem, XA_DIM), lambda i, j: (i, 0, 0))],
        out_specs=pl.BlockSpec((nb, tq, XA_DIM), lambda i, j: (i, j, 0)),
        out_shape=jax.ShapeDtypeStruct((b, l, XA_DIM), out_dtype),
        compiler_params=_params("parallel", "arbitrary"),
        name="cross_attention",
    )(q, k, v)


def _attn_sample_kernel(q_ref, k_ref, v_ref, bias_ref, o_ref, *, nb):
    scale = XA_HEAD_DIM ** -0.5
    for b in range(nb):
        q = q_ref[b].astype(BF16)
        k = k_ref[b].astype(BF16)
        v = v_ref[b].astype(BF16)
        s = lax.dot_general(q, k, (((1,), (1,)), ((), ())), preferred_element_type=F32) * scale + bias_ref[...]
        e = jnp.exp(s - jnp.max(s, axis=-1, keepdims=True))
        p = e / jnp.sum(e, axis=-1, keepdims=True)
        o_ref[b] = jnp.dot(p.astype(BF16), v, preferred_element_type=F32)


def cross_attention_sample(q, cache_k, cache_v, bias, layer, *, nb=8):
    b, nq, dh = q.shape
    nkv = cache_k.shape[2]
    kv = pl.BlockSpec((None, nb, nkv, dh), lambda i: (layer, i, 0, 0))
    return pl.pallas_call(
        functools.partial(_attn_sample_kernel, nb=nb),
        grid=(b // nb,),
        in_specs=[pl.BlockSpec((nb, nq, dh), lambda i: (i, 0, 0)), kv, kv, _resident((nq, nkv))],
        out_specs=pl.BlockSpec((nb, nq, dh), lambda i: (i, 0, 0)),
        out_shape=jax.ShapeDtypeStruct((b, nq, dh), F32),
        compiler_params=_params("parallel"),
        name="cross_attention_sample",
    )(q, cache_k, cache_v, bias)


def _out_proj_kernel(y_ref, xa_ref, w1_ref, w2_ref, x_ref, fg_ref, o_ref, *, final_norm):
    acc = x_ref[...] + jnp.dot(y_ref[...], w1_ref[...], preferred_element_type=F32)
    acc = acc + jnp.dot(xa_ref[...], w2_ref[...], preferred_element_type=F32)
    o_ref[...] = _rms(acc, fg_ref[...]) if final_norm else acc


def out_proj(y, xa, w1, w2, x, final_g, *, final_norm, tm=512):
    m, d = x.shape
    e, a = y.shape[1], xa.shape[1]
    tm = min(tm, m)
    return pl.pallas_call(
        functools.partial(_out_proj_kernel, final_norm=final_norm),
        grid=(m // tm,),
        in_specs=[pl.BlockSpec((tm, e), lambda i: (i, 0)),
                  pl.BlockSpec((tm, a), lambda i: (i, 0)),
                  _resident((e, d)),
                  _resident((a, d)),
                  pl.BlockSpec((tm, d), lambda i: (i, 0)),
                  _resident((1, d))],
        out_specs=pl.BlockSpec((tm, d), lambda i: (i, 0)),
        out_shape=jax.ShapeDtypeStruct((m, d), F32),
        compiler_params=_params("parallel"),
        name="out_proj",
    )(y, xa, w1, w2, x, final_g.reshape(1, d))


def _sgu_sample_kernel(u_ref, v_ref, z_ref, g_ref, coef_ref, bias_ref, o_ref, vn_ref):
    steps = u_ref.shape[0]
    vn = []
    for t in range(steps):
        n = _rms(v_ref[t], g_ref[...])
        vn_ref[t] = n
        vn.append(n)
    for t in range(steps):
        mixed = bias_ref[t:t + 1, :]
        for s in range(t + 1):
            mixed = mixed + coef_ref[t * steps + s:t * steps + s + 1, :] * vn[s]
        o_ref[t] = (u_ref[t] * mixed * _silu(z_ref[t])).astype(BF16)


def sgu_sample(proj, g, coef, bias, *, e, nbb=32):
    t, b, _ = proj.shape
    col = lambda c: pl.BlockSpec((t, nbb, e), lambda i: (0, i, c))
    out = pl.BlockSpec((t, nbb, e), lambda i: (0, i, 0))
    return pl.pallas_call(
        _sgu_sample_kernel,
        grid=(b // nbb,),
        in_specs=[col(0), col(1), col(2), _resident((1, e)), _resident(coef.shape), _resident(bias.shape)],
        out_specs=[out, out],
        out_shape=[jax.ShapeDtypeStruct((t, b, e), BF16), jax.ShapeDtypeStruct((t, b, e), F32)],
        compiler_params=_params("parallel"),
        name="sgu_sample",
    )(proj, proj, proj, g.reshape(1, e), coef, bias)


def _conv_sample_kernel(bg_ref, cg_ref, hv_ref, z_ref, w_ref, st_ref, o_ref, ns_ref):
    steps = cg_ref.shape[0]
    xp = [st_ref[k] for k in range(CONV_W - 1)] + [cg_ref[t] * hv_ref[t] for t in range(steps)]
    for t in range(steps):
        y = w_ref[0:1, :] * xp[t]
        for k in range(1, CONV_W):
            y = y + w_ref[k:k + 1, :] * xp[t + k]
        o_ref[t] = (bg_ref[t] * y * _silu(z_ref[t])).astype(BF16)
    for k in range(CONV_W - 1):
        ns_ref[k] = xp[steps + k]


def conv_sample(proj, w, state, *, e, nbb=32):
    t, b, _ = proj.shape
    col = lambda c: pl.BlockSpec((t, nbb, e), lambda i: (0, i, c))
    st = pl.BlockSpec((CONV_W - 1, nbb, e), lambda i: (0, i, 0))
    return pl.pallas_call(
        _conv_sample_kernel,
        grid=(b // nbb,),
        in_specs=[col(0), col(1), col(2), col(3), _resident((CONV_W, e)), st],
        out_specs=[pl.BlockSpec((t, nbb, e), lambda i: (0, i, 0)), st],
        out_shape=[jax.ShapeDtypeStruct((t, b, e), BF16), jax.ShapeDtypeStruct((CONV_W - 1, b, e), F32)],
        compiler_params=_params("parallel"),
        name="conv_sample",
    )(proj, proj, proj, proj, w, state)


LAYER_COLS = 256


def _attend_into(q, k_ref, v_ref, y_ref, col0):
    scale = XA_HEAD_DIM ** -0.5
    qb = q.astype(BF16)
    for h in range(XA_HEADS):
        hs = slice(h * XA_HEAD_DIM, (h + 1) * XA_HEAD_DIM)
        s = lax.dot_general(qb[:, hs], k_ref[:, hs], (((1,), (1,)), ((), ())),
                            preferred_element_type=F32) * scale
        ex = jnp.exp(s - jnp.max(s, axis=-1, keepdims=True))
        p = ex / jnp.sum(ex, axis=-1, keepdims=True)
        o = jnp.dot(p.astype(BF16), v_ref[:, hs], preferred_element_type=F32)
        y_ref[:, col0 + h * XA_HEAD_DIM:col0 + (h + 1) * XA_HEAD_DIM] = o.astype(BF16)


def _residual_out(x, y_ref, wo_ref, fg_ref, o_ref, final_norm):
    acc = x + jnp.dot(y_ref[...], wo_ref[...], preferred_element_type=F32)
    o_ref[...] = _rms(acc, fg_ref[...]) if final_norm else acc


def _sgu_layer_kernel(x_ref, ng_ref, w_ref, sg_ref, ws_ref, bias_ref, k_ref, v_ref, wo_ref, fg_ref, o_ref,
                      hn_ref, vn_ref, y_ref, *, e, final_norm):
    tm = x_ref.shape[0]
    gd = e // SGU_GROUPS
    x = x_ref[...]
    hn_ref[...] = _rms(x, ng_ref[...]).astype(BF16)
    hn = hn_ref[...]

    for c0 in range(0, e, LAYER_COLS):
        vn_ref[:, c0:c0 + LAYER_COLS] = jnp.dot(hn, w_ref[:, e + c0:e + c0 + LAYER_COLS],
                                                preferred_element_type=F32)
    vn_ref[...] = _rms(vn_ref[...], sg_ref[...])

    row = lax.broadcasted_iota(jnp.int32, (CHUNK, CHUNK), 0)
    col = lax.broadcasted_iota(jnp.int32, (CHUNK, CHUNK), 1)
    for g in range(SGU_GROUPS):
        cs = slice(g * gd, (g + 1) * gd)
        wg = jnp.where(row >= col, ws_ref[g], 0.0).astype(BF16)
        u = jnp.dot(hn, w_ref[:, g * gd:(g + 1) * gd], preferred_element_type=F32)
        z = jnp.dot(hn, w_ref[:, 2 * e + g * gd:2 * e + (g + 1) * gd], preferred_element_type=F32)
        for c in range(tm // CHUNK):
            rs = slice(c * CHUNK, (c + 1) * CHUNK)
            mixed = jnp.dot(wg, vn_ref[rs, cs].astype(BF16), preferred_element_type=F32) + bias_ref[:, cs]
            y_ref[rs, cs] = (u[rs] * mixed * _silu(z[rs])).astype(BF16)

    q = jnp.dot(hn, w_ref[:, 3 * e:3 * e + XA_DIM], preferred_element_type=F32)
    _attend_into(q, k_ref, v_ref, y_ref, e)
    _residual_out(x, y_ref, wo_ref, fg_ref, o_ref, final_norm)


def sgu_layer(x, ng, w, sg, ws, bias, kb, vb, layer, wo, fg, *, b, e, final_norm, tm=256):
    m, d = x.shape
    lt = m // b // tm
    mem = kb.shape[2]
    kv = pl.BlockSpec((None, None, mem, XA_DIM), lambda bi, i: (layer, bi, 0, 0))
    return pl.pallas_call(
        functools.partial(_sgu_layer_kernel, e=e, final_norm=final_norm),
        grid=(b, lt),
        in_specs=[pl.BlockSpec((tm, d), lambda bi, i: (bi * lt + i, 0)),
                  _resident((1, d)), _resident(w.shape), _resident((1, e)), _resident(ws.shape),
                  _resident(bias.shape), kv, kv, _resident(wo.shape), _resident((1, d))],
        out_specs=pl.BlockSpec((tm, d), lambda bi, i: (bi * lt + i, 0)),
        out_shape=jax.ShapeDtypeStruct((m, d), F32),
        scratch_shapes=[pltpu.VMEM((tm, d), BF16), pltpu.VMEM((tm, e), F32),
                        pltpu.VMEM((tm, e + XA_DIM), BF16)],
        compiler_params=_params("parallel", "arbitrary"),
        name="sgu_layer",
    )(x, ng.reshape(1, d), w, sg.reshape(1, e), ws, bias, kb, vb, wo, fg.reshape(1, d))


def _conv_layer_kernel(x_ref, ng_ref, w_ref, cw_ref, init_ref, k_ref, v_ref, wo_ref, fg_ref, o_ref, st_ref,
                       hn_ref, y_ref, carry_ref, *, e, final_norm):
    tm = x_ref.shape[0]
    keep = CONV_W - 1
    edge = carry_ref.shape[0]

    @pl.when(pl.program_id(1) == 0)
    def _():
        carry_ref[...] = jnp.zeros_like(carry_ref)
        carry_ref[edge - keep:edge, :] = init_ref[...]

    x = x_ref[...]
    hn_ref[...] = _rms(x, ng_ref[...]).astype(BF16)
    hn = hn_ref[...]
    rid = lax.broadcasted_iota(jnp.int32, (tm, LAYER_COLS), 0)
    for c0 in range(0, e, LAYER_COLS):
        cs = slice(c0, c0 + LAYER_COLS)
        part = lambda k: jnp.dot(hn, w_ref[:, k * e + c0:k * e + c0 + LAYER_COLS], preferred_element_type=F32)
        xc = part(1) * part(2)
        p1 = carry_ref[edge - 1:edge, cs]
        p2 = carry_ref[edge - 2:edge - 1, cs]
        x1 = jnp.where(rid == 0, p1, pltpu.roll(xc, 1, 0))
        x2 = jnp.where(rid == 0, p2, jnp.where(rid == 1, p1, pltpu.roll(xc, 2, 0)))
        y = cw_ref[0:1, cs] * x2 + cw_ref[1:2, cs] * x1 + cw_ref[2:3, cs] * xc
        y_ref[:, cs] = (part(0) * y * _silu(part(3))).astype(BF16)
        carry_ref[:, cs] = xc[tm - edge:, :]
        st_ref[:, cs] = xc[tm - keep:, :]

    q = jnp.dot(hn, w_ref[:, 4 * e:4 * e + XA_DIM], preferred_element_type=F32)
    _attend_into(q, k_ref, v_ref, y_ref, e)
    _residual_out(x, y_ref, wo_ref, fg_ref, o_ref, final_norm)


def conv_layer(x, ng, w, cw, init, kb, vb, layer, wo, fg, *, b, e, final_norm, tm=256):
    m, d = x.shape
    lt = m // b // tm
    mem = kb.shape[2]
    kv = pl.BlockSpec((None, None, mem, XA_DIM), lambda bi, i: (layer, bi, 0, 0))
    st = pl.BlockSpec((None, CONV_W - 1, e), lambda bi, i: (bi, 0, 0))
    return pl.pallas_call(
        functools.partial(_conv_layer_kernel, e=e, final_norm=final_norm),
        grid=(b, lt),
        in_specs=[pl.BlockSpec((tm, d), lambda bi, i: (bi * lt + i, 0)),
                  _resident((1, d)), _resident(w.shape), _resident((CONV_W, e)), st,
                  kv, kv, _resident(wo.shape), _resident((1, d))],
        out_specs=[pl.BlockSpec((tm, d), lambda bi, i: (bi * lt + i, 0)), st],
        out_shape=[jax.ShapeDtypeStruct((m, d), F32), jax.ShapeDtypeStruct((b, CONV_W - 1, e), F32)],
        scratch_shapes=[pltpu.VMEM((tm, d), BF16), pltpu.VMEM((tm, e + XA_DIM), BF16),
                        pltpu.VMEM((V7X_SUBLANES, e), F32)],
        compiler_params=_params("parallel", "arbitrary"),
        name="conv_layer",
    )(x, ng.reshape(1, d), w, cw, init, kb, vb, wo, fg.reshape(1, d))


def _s5_disc_kernel(lre_ref, lim_ref, ldt_ref, bre_ref, bim_ref, ar_ref, ai_ref, bbr_ref, bbi_ref):
    dt = jnp.exp(ldt_ref[...])
    lr, li = lre_ref[...], lim_ref[...]
    mag = jnp.exp(lr * dt)
    ar = mag * jnp.cos(li * dt)
    ai = mag * jnp.sin(li * dt)
    nr, ni = ar - 1, ai
    den = lr * lr + li * li
    cr = (nr * lr + ni * li) / den
    ci = (ni * lr - nr * li) / den
    ar_ref[...] = ar
    ai_ref[...] = ai
    cr3, ci3 = cr[:, None, :], ci[:, None, :]
    bbr_ref[...] = cr3 * bre_ref[...] - ci3 * bim_ref[...]
    bbi_ref[...] = cr3 * bim_ref[...] + ci3 * bre_ref[...]


def s5_discretize(lam_re, lam_im, log_dt, b_re, b_im):
    g, p = lam_re.shape
    gp = jax.ShapeDtypeStruct((g, p), F32)
    gip = jax.ShapeDtypeStruct(b_re.shape, F32)
    return pl.pallas_call(
        _s5_disc_kernel,
        out_shape=[gp, gp, gip, gip],
        compiler_params=pltpu.CompilerParams(vmem_limit_bytes=V7X_VMEM_LIMIT_BYTES),
        name="s5_discretize",
    )(lam_re, lam_im, log_dt.reshape(g, 1), b_re, b_im)


def _s5_kernel(u_ref, bre_ref, bim_ref, cre_ref, cim_ref, ar_ref, ai_ref, d_ref, h0r_ref, h0i_ref,
               g_ref, hr_ref, hi_ref, sr_ref, si_ref):
    tl, nb, e = u_ref.shape
    rows = tl * nb
    n_blk = bre_ref.shape[0]
    fk = bre_ref.shape[1]
    sk = bre_ref.shape[2]
    lanes = sr_ref.shape[1]

    @pl.when(pl.program_id(1) == 0)
    def _():
        hr_ref[...] = h0r_ref[...]
        hi_ref[...] = h0i_ref[...]

    u = u_ref[...].reshape(rows, e)
    ub = u.astype(BF16)
    for kb in range(n_blk):
        uk = ub[:, kb * fk:(kb + 1) * fk]
        sr_ref[:, kb * sk:(kb + 1) * sk] = jnp.dot(uk, bre_ref[kb], preferred_element_type=F32)
        si_ref[:, kb * sk:(kb + 1) * sk] = jnp.dot(uk, bim_ref[kb], preferred_element_type=F32)

    for c in range(lanes // S5_SCAN_LANES):
        ls = slice(c * S5_SCAN_LANES, (c + 1) * S5_SCAN_LANES)
        a_r = jnp.broadcast_to(ar_ref[:, ls], (V7X_SUBLANES, S5_SCAN_LANES))
        a_i = jnp.broadcast_to(ai_ref[:, ls], (V7X_SUBLANES, S5_SCAN_LANES))

        def sweep(sg, _):
            r0 = pl.multiple_of(sg * V7X_SUBLANES, V7X_SUBLANES)

            def step(t, h):
                hr, hi = h
                row = pl.multiple_of(t * nb + r0, V7X_SUBLANES)
                nr = a_r * hr - a_i * hi + sr_ref[pl.ds(row, V7X_SUBLANES), ls]
                ni = a_r * hi + a_i * hr + si_ref[pl.ds(row, V7X_SUBLANES), ls]
                sr_ref[pl.ds(row, V7X_SUBLANES), ls] = nr
                si_ref[pl.ds(row, V7X_SUBLANES), ls] = ni
                return nr, ni

            h0 = (hr_ref[pl.ds(r0, V7X_SUBLANES), ls], hi_ref[pl.ds(r0, V7X_SUBLANES), ls])
            hr, hi = lax.fori_loop(0, tl, step, h0, unroll=min(tl, 8))
            hr_ref[pl.ds(r0, V7X_SUBLANES), ls] = hr
            hi_ref[pl.ds(r0, V7X_SUBLANES), ls] = hi
            return 0

        lax.fori_loop(0, nb // V7X_SUBLANES, sweep, 0)

    for kb in range(n_blk):
        hrb = sr_ref[:, kb * sk:(kb + 1) * sk].astype(BF16)
        hib = si_ref[:, kb * sk:(kb + 1) * sk].astype(BF16)
        y = jnp.dot(hrb, cre_ref[kb], preferred_element_type=F32)
        y = y - jnp.dot(hib, cim_ref[kb], preferred_element_type=F32)
        fs = slice(kb * fk, (kb + 1) * fk)
        y = y + d_ref[:, fs] * u[:, fs]
        g_ref[:, :, fs] = jax.nn.gelu(y).reshape(tl, nb, fk)


def s5_scan(proj, bre, bim, cre, cim, ar, ai, d, h0r, h0i, *, e, tl, nbb):
    l, b, _ = proj.shape
    lanes = h0r.shape[1]
    rows = tl * nbb
    st = pl.BlockSpec((nbb, lanes), lambda bi, i: (bi, 0))
    return pl.pallas_call(
        _s5_kernel,
        grid=(b // nbb, l // tl),
        in_specs=[pl.BlockSpec((tl, nbb, e), lambda bi, i: (i, bi, 0)),
                  _resident(bre.shape), _resident(bim.shape), _resident(cre.shape), _resident(cim.shape),
                  _resident((1, lanes)), _resident((1, lanes)), _resident((1, e)), st, st],
        out_specs=[pl.BlockSpec((tl, nbb, e), lambda bi, i: (i, bi, 0)), st, st],
        out_shape=[jax.ShapeDtypeStruct((l, b, e), F32),
                   jax.ShapeDtypeStruct((b, lanes), F32), jax.ShapeDtypeStruct((b, lanes), F32)],
        scratch_shapes=[pltpu.VMEM((rows, lanes), F32), pltpu.VMEM((rows, lanes), F32)],
        compiler_params=_params("parallel", "arbitrary"),
        name="s5_scan",
    )(proj, bre, bim, cre, cim, ar.reshape(1, lanes), ai.reshape(1, lanes), d.reshape(1, e), h0r, h0i)


def _glu_kernel(g_ref, w_ref, b_ref, gj_ref, z_ref, o_ref, gb_ref):
    @pl.when(pl.program_id(1) == 0)
    def _():
        gb_ref[...] = g_ref[...].astype(BF16)

    a = jnp.dot(gb_ref[...], w_ref[...], preferred_element_type=F32) + b_ref[...]
    o_ref[...] = (gj_ref[...] * jax.nn.sigmoid(a) * _silu(z_ref[...])).astype(BF16)


def glu_gate(g, w, bias, proj, *, e, tm=512, tn=512):
    m = g.shape[0]
    tm = min(tm, m)
    return pl.pallas_call(
        _glu_kernel,
        grid=(m // tm, e // tn),
        in_specs=[pl.BlockSpec((tm, e), lambda i, j: (i, 0)),
                  pl.BlockSpec((e, tn), lambda i, j: (0, j)),
                  pl.BlockSpec((1, tn), lambda i, j: (0, j)),
                  pl.BlockSpec((tm, tn), lambda i, j: (i, j)),
                  pl.BlockSpec((tm, tn), lambda i, j: (i, e // tn + j))],
        out_specs=pl.BlockSpec((tm, tn), lambda i, j: (i, j)),
        out_shape=jax.ShapeDtypeStruct((m, e), BF16),
        scratch_shapes=[pltpu.VMEM((tm, e), BF16)],
        compiler_params=_params("parallel", "arbitrary"),
        name="glu_gate",
    )(g, w, bias.reshape(1, e), g, proj)


def _block_diag(x, n_blk):
    g, r, c = x.shape
    gb = g // n_blk
    eye = jnp.eye(gb, dtype=x.dtype)
    out = jnp.einsum("bgrc,gh->bgrhc", x.reshape(n_blk, gb, r, c), eye)
    return out.reshape(n_blk, gb * r, gb * c)


def kernel(x_prompt, x_sample, mem_prompt, cache_mem_k, cache_mem_v, state_conv, state_s5_re, state_s5_im,
           norm_g, final_g, mem_norm_g, w_kv, w_out, w_in_a, sgu_norm_g, sgu_w, sgu_b, w_in_b, conv_w, w_in_c,
           s5_lam_re, s5_lam_im, s5_log_dt, s5_b_re, s5_b_im, s5_c_re, s5_c_im, s5_d, w_glu, b_glu):
    bp, lp, d = x_prompt.shape
    bs, ls, _ = x_sample.shape
    depth = norm_g.shape[0]
    e = conv_w.shape[-1]
    mem_len = mem_prompt.shape[1]
    n_grp, n_state = s5_lam_re.shape[1:]
    gd = e // SGU_GROUPS
    n_blk = e // S5_FEATS_PER_BLOCK
    kinds = [i % N_MIXERS for i in range(depth)]
    kidx = [sum(1 for j in range(i) if kinds[j] == kinds[i]) for i in range(depth)]

    mk, mv, mkb, mvb = mem_kv(mem_prompt.reshape(bp * mem_len, d), mem_norm_g, w_kv.astype(BF16))
    mem_k_prompt = mk.reshape(depth, bp, mem_len, XA_HEADS, XA_HEAD_DIM)
    mem_v_prompt = mv.reshape(depth, bp, mem_len, XA_HEADS, XA_HEAD_DIM)
    mkb = mkb.reshape(depth, bp, mem_len, XA_DIM)
    mvb = mvb.reshape(depth, bp, mem_len, XA_DIM)
    ck = cache_mem_k.reshape(depth, bs, mem_len * XA_HEADS, XA_HEAD_DIM)
    cv = cache_mem_v.reshape(depth, bs, mem_len * XA_HEADS, XA_HEAD_DIM)
    q_head = jnp.arange(XA_HEADS * ls, dtype=jnp.int32)[:, None] // ls
    kv_head = jnp.arange(mem_len * XA_HEADS, dtype=jnp.int32)[None, :] % XA_HEADS
    head_bias = jnp.where(q_head == kv_head, 0.0, -1e30).astype(F32)

    xp = x_prompt.reshape(bp * lp, d)
    xs = x_sample.transpose(1, 0, 2).reshape(ls * bs, d)

    def sample_attention(proj, qcol, i):
        q = proj.reshape(ls, bs, -1)[:, :, qcol * XA_DIM:(qcol + 1) * XA_DIM]
        q = q.reshape(ls, bs, XA_HEADS, XA_HEAD_DIM).transpose(1, 2, 0, 3).reshape(bs, XA_HEADS * ls, XA_HEAD_DIM)
        xa = cross_attention_sample(q, ck, cv, head_bias, i)
        xa = xa.reshape(bs, XA_HEADS, ls, XA_HEAD_DIM).transpose(2, 0, 1, 3)
        return xa.reshape(ls * bs, XA_DIM).astype(BF16)

    conv_p, conv_s, s5r_p, s5i_p, s5r_s, s5i_s, chunk_v = [], [], [], [], [], [], []
    for i in range(depth):
        kind, j = kinds[i], kidx[i]
        wo = w_out[i].astype(BF16)
        w1, w2 = wo[:e], wo[e:]
        last = i == depth - 1
        if kind == 0:
            w = w_in_a[j].astype(BF16)
            qcol = 3 * e // XA_DIM
            bias = jnp.repeat(sgu_b[j].T, gd, axis=1)
            xp = sgu_layer(xp, norm_g[i], w, sgu_norm_g[j], sgu_w[j], bias, mkb, mvb, i, wo, final_g,
                           b=bp, e=e, final_norm=last)
            ps = norm_proj(xs, norm_g[i], w, tm=512)
            coef = jnp.repeat(sgu_w[j][:, :ls, :ls].transpose(1, 2, 0).reshape(ls * ls, SGU_GROUPS), gd, axis=1)
            ys, vn = sgu_sample(ps.reshape(ls, bs, -1), sgu_norm_g[j], coef, bias[:V7X_SUBLANES], e=e)
            chunk_v.append(vn.transpose(1, 0, 2))
            ys = ys.reshape(ls * bs, e)
            as_ = sample_attention(ps, qcol, i)
        elif kind == 1:
            w = w_in_b[j].astype(BF16)
            qcol = 4 * e // XA_DIM
            xp, cst = conv_layer(xp, norm_g[i], w, conv_w[j], jnp.zeros((bp, CONV_W - 1, e), F32), mkb, mvb, i,
                                 wo, final_g, b=bp, e=e, final_norm=last)
            conv_p.append(cst)
            ps = norm_proj(xs, norm_g[i], w, tm=512)
            ys, nst = conv_sample(ps.reshape(ls, bs, -1), conv_w[j], state_conv[j].transpose(1, 0, 2), e=e)
            conv_s.append(nst.transpose(1, 0, 2))
            ys = ys.reshape(ls * bs, e)
            as_ = sample_attention(ps, qcol, i)
        else:
            wc = w_in_c[j].astype(BF16)
            w_uz, w_q = wc[:, :2 * e], wc[:, 2 * e:]
            ar, ai, bbr, bbi = s5_discretize(s5_lam_re[j], s5_lam_im[j], s5_log_dt[j],
                                             s5_b_re[j].transpose(0, 2, 1), s5_b_im[j].transpose(0, 2, 1))
            bre = _block_diag(bbr, n_blk).astype(BF16)
            bim = _block_diag(bbi, n_blk).astype(BF16)
            cre = _block_diag(s5_c_re[j].transpose(0, 2, 1), n_blk).astype(BF16)
            cim = _block_diag(s5_c_im[j].transpose(0, 2, 1), n_blk).astype(BF16)
            ar, ai = ar.reshape(-1), ai.reshape(-1)

            xt = xp.reshape(bp, lp, d).transpose(1, 0, 2).reshape(lp * bp, d)
            pt = norm_proj(xt, norm_g[i], w_uz, tm=2048)
            pq = norm_proj(xp, norm_g[i], w_q, tm=2048)
            zero = jnp.zeros((bp, n_grp * n_state), F32)
            gp, hr, hi = s5_scan(pt.reshape(lp, bp, -1), bre, bim, cre, cim, ar, ai, s5_d[j], zero, zero,
                                 e=e, tl=32, nbb=bp)
            s5r_p.append(hr.reshape(bp, n_grp, n_state))
            s5i_p.append(hi.reshape(bp, n_grp, n_state))
            yt = glu_gate(gp.reshape(lp * bp, e), w_glu[j].astype(BF16), b_glu[j], pt, e=e)
            ap = cross_attention(pq.reshape(bp, lp, -1), 0, mkb[i], mvb[i], nb=1, tq=512, out_dtype=BF16)
            at = ap.transpose(1, 0, 2).reshape(lp * bp, XA_DIM)
            xt = out_proj(yt, at, w1, w2, xt, final_g, final_norm=last)
            xp = xt.reshape(lp, bp, d).transpose(1, 0, 2).reshape(bp * lp, d)

            ps = norm_proj(xs, norm_g[i], wc, tm=512)
            gs, hr, hi = s5_scan(ps.reshape(ls, bs, -1), bre, bim, cre, cim, ar, ai, s5_d[j],
                                 state_s5_re[j].reshape(bs, -1), state_s5_im[j].reshape(bs, -1),
                                 e=e, tl=ls, nbb=32)
            s5r_s.append(hr.reshape(bs, n_grp, n_state))
            s5i_s.append(hi.reshape(bs, n_grp, n_state))
            ys = glu_gate(gs.reshape(ls * bs, e), w_glu[j].astype(BF16), b_glu[j], ps, e=e)
            as_ = sample_attention(ps, 2 * e // XA_DIM, i)
        xs = out_proj(ys, as_, w1, w2, xs, final_g, final_norm=last)

    y_prompt = xp.reshape(bp, lp, d)
    y_sample = xs.reshape(ls, bs, d).transpose(1, 0, 2)
    return (y_prompt, y_sample, mem_k_prompt, mem_v_prompt, jnp.stack(conv_p), jnp.stack(conv_s),
            jnp.stack(s5r_p), jnp.stack(s5i_p), jnp.stack(s5r_s), jnp.stack(s5i_s), jnp.stack(chunk_v))
```

```python
import functools

import jax
import jax.numpy as jnp
from jax import lax
from jax.experimental import pallas as pl
from jax.experimental.pallas import tpu as pltpu

F32 = jnp.float32
BF16 = jnp.bfloat16

EPS = 1e-6
N_MIXERS = 3
CHUNK = 128
SGU_GROUPS = 8
CONV_W = 3
S5_IN = 16
XA_HEADS = 4
XA_HEAD_DIM = 128
XA_DIM = XA_HEADS * XA_HEAD_DIM

V7X_SUBLANES = 8
V7X_VMEM_LIMIT_BYTES = 56 * 1024 * 1024
S5_FEATS_PER_BLOCK = 256
S5_SCAN_LANES = 512


def _params(*semantics):
    return pltpu.CompilerParams(dimension_semantics=semantics, vmem_limit_bytes=V7X_VMEM_LIMIT_BYTES)


def _resident(shape):
    nd = len(shape)
    return pl.BlockSpec(shape, lambda *_: (0,) * nd, pipeline_mode=pl.Buffered(1))


def _rms(x, g):
    return x * lax.rsqrt(jnp.mean(x * x, axis=-1, keepdims=True) + EPS) * g


def _silu(z):
    return z * jax.nn.sigmoid(z)


def _norm_proj_kernel(x_ref, g_ref, w_ref, o_ref, hn_ref):
    @pl.when(pl.program_id(1) == 0)
    def _():
        hn_ref[...] = _rms(x_ref[...], g_ref[...]).astype(BF16)

    o_ref[...] = jnp.dot(hn_ref[...], w_ref[...], preferred_element_type=F32)


def norm_proj(x, g, w, *, tm, tn=512):
    m, d = x.shape
    n = w.shape[1]
    tm = min(tm, m)
    return pl.pallas_call(
        _norm_proj_kernel,
        grid=(m // tm, n // tn),
        in_specs=[pl.BlockSpec((tm, d), lambda i, j: (i, 0)),
                  pl.BlockSpec((1, d), lambda i, j: (0, 0)),
                  pl.BlockSpec((d, tn), lambda i, j: (0, j))],
        out_specs=pl.BlockSpec((tm, tn), lambda i, j: (i, j)),
        out_shape=jax.ShapeDtypeStruct((m, n), F32),
        scratch_shapes=[pltpu.VMEM((tm, d), BF16)],
        compiler_params=_params("parallel", "arbitrary"),
        name="norm_proj",
    )(x, g.reshape(1, d), w)


def _mem_kv_kernel(x_ref, g_ref, w_ref, k_ref, v_ref, kb_ref, vb_ref):
    hn = _rms(x_ref[...], g_ref[0]).astype(BF16)
    kv = jnp.dot(hn, w_ref[0], preferred_element_type=F32)
    k, v = kv[:, :XA_DIM], kv[:, XA_DIM:]
    k_ref[0] = k
    v_ref[0] = v
    kb_ref[0] = k.astype(BF16)
    vb_ref[0] = v.astype(BF16)


def mem_kv(mem, g, w, *, tm=512):
    m, d = mem.shape
    depth = g.shape[0]
    out = pl.BlockSpec((1, tm, XA_DIM), lambda l, i: (l, i, 0))
    shape = lambda dt: jax.ShapeDtypeStruct((depth, m, XA_DIM), dt)
    return pl.pallas_call(
        _mem_kv_kernel,
        grid=(depth, m // tm),
        in_specs=[pl.BlockSpec((tm, d), lambda l, i: (i, 0)),
                  pl.BlockSpec((1, 1, d), lambda l, i: (l, 0, 0)),
                  pl.BlockSpec((1, d, 2 * XA_DIM), lambda l, i: (l, 0, 0))],
        out_specs=[out, out, out, out],
        out_shape=[shape(F32), shape(F32), shape(BF16), shape(BF16)],
        compiler_params=_params("parallel", "parallel"),
        name="mem_kv",
    )(mem, g.reshape(depth, 1, d), w)


def _attn_kernel(q_ref, k_ref, v_ref, o_ref, *, nb):
    scale = XA_HEAD_DIM ** -0.5
    for b in range(nb):
        q = q_ref[b].astype(BF16)
        k = k_ref[b].astype(BF16)
        v = v_ref[b].astype(BF16)
        for h in range(XA_HEADS):
            hs = slice(h * XA_HEAD_DIM, (h + 1) * XA_HEAD_DIM)
            s = lax.dot_general(q[:, hs], k[:, hs], (((1,), (1,)), ((), ())),
                                preferred_element_type=F32) * scale
            e = jnp.exp(s - jnp.max(s, axis=-1, keepdims=True))
            p = e / jnp.sum(e, axis=-1, keepdims=True)
            o = jnp.dot(p.astype(BF16), v[:, hs], preferred_element_type=F32)
            o_ref[b, :, hs] = o.astype(o_ref.dtype)


def cross_attention(q, qcol, k, v, *, nb, tq, out_dtype):
    b, l, _ = q.shape
    mem = k.shape[1]
    return pl.pallas_call(
        functools.partial(_attn_kernel, nb=nb),
        grid=(b // nb, l // tq),
        in_specs=[pl.BlockSpec((nb, tq, XA_DIM), lambda i, j: (i, j, qcol)),
                  pl.BlockSpec((nb, mem, XA_DIM), lambda i, j: (i, 0, 0)),
                  pl.BlockSpec((nb, mem, XA_DIM), lambda i, j: (i, 0, 0))],
        out_specs=pl.BlockSpec((nb, tq, XA_DIM), lambda i, j: (i, j, 0)),
        out_shape=jax.ShapeDtypeStruct((b, l, XA_DIM), out_dtype),
        compiler_params=_params("parallel", "arbitrary"),
        name="cross_attention",
    )(q, k, v)


def _attn_sample_kernel(q_ref, k_ref, v_ref, bias_ref, o_ref, *, nb):
    scale = XA_HEAD_DIM ** -0.5
    for b in range(nb):
        q = q_ref[b].astype(BF16)
        k = k_ref[b].astype(BF16)
        v = v_ref[b].astype(BF16)
        s = lax.dot_general(q, k, (((1,), (1,)), ((), ())), preferred_element_type=F32) * scale + bias_ref[...]
        e = jnp.exp(s - jnp.max(s, axis=-1, keepdims=True))
        p = e / jnp.sum(e, axis=-1, keepdims=True)
        o_ref[b] = jnp.dot(p.astype(BF16), v, preferred_element_type=F32)


def cross_attention_sample(q, cache_k, cache_v, bias, layer, *, nb=8):
    b, nq, dh = q.shape
    nkv = cache_k.shape[2]
    kv = pl.BlockSpec((None, nb, nkv, dh), lambda i: (layer, i, 0, 0))
    return pl.pallas_call(
        functools.partial(_attn_sample_kernel, nb=nb),
        grid=(b // nb,),
        in_specs=[pl.BlockSpec((nb, nq, dh), lambda i: (i, 0, 0)), kv, kv, _resident((nq, nkv))],
        out_specs=pl.BlockSpec((nb, nq, dh), lambda i: (i, 0, 0)),
        out_shape=jax.ShapeDtypeStruct((b, nq, dh), F32),
        compiler_params=_params("parallel"),
        name="cross_attention_sample",
    )(q, cache_k, cache_v, bias)


def _out_proj_kernel(y_ref, xa_ref, w1_ref, w2_ref, x_ref, fg_ref, o_ref, *, final_norm):
    acc = x_ref[...] + jnp.dot(y_ref[...], w1_ref[...], preferred_element_type=F32)
    acc = acc + jnp.dot(xa_ref[...], w2_ref[...], preferred_element_type=F32)
    o_ref[...] = _rms(acc, fg_ref[...]) if final_norm else acc


def out_proj(y, xa, w1, w2, x, final_g, *, final_norm, tm=512):
    m, d = x.shape
    e, a = y.shape[1], xa.shape[1]
    tm = min(tm, m)
    return pl.pallas_call(
        functools.partial(_out_proj_kernel, final_norm=final_norm),
        grid=(m // tm,),
        in_specs=[pl.BlockSpec((tm, e), lambda i: (i, 0)),
                  pl.BlockSpec((tm, a), lambda i: (i, 0)),
                  _resident((e, d)),
                  _resident((a, d)),
                  pl.BlockSpec((tm, d), lambda i: (i, 0)),
                  _resident((1, d))],
        out_specs=pl.BlockSpec((tm, d), lambda i: (i, 0)),
        out_shape=jax.ShapeDtypeStruct((m, d), F32),
        compiler_params=_params("parallel"),
        name="out_proj",
    )(y, xa, w1, w2, x, final_g.reshape(1, d))


def _sgu_sample_kernel(u_ref, v_ref, z_ref, g_ref, coef_ref, bias_ref, o_ref, vn_ref):
    steps = u_ref.shape[0]
    vn = []
    for t in range(steps):
        n = _rms(v_ref[t], g_ref[...])
        vn_ref[t] = n
        vn.append(n)
    for t in range(steps):
        mixed = bias_ref[t:t + 1, :]
        for s in range(t + 1):
            mixed = mixed + coef_ref[t * steps + s:t * steps + s + 1, :] * vn[s]
        o_ref[t] = (u_ref[t] * mixed * _silu(z_ref[t])).astype(BF16)


def sgu_sample(proj, g, coef, bias, *, e, nbb=32):
    t, b, _ = proj.shape
    col = lambda c: pl.BlockSpec((t, nbb, e), lambda i: (0, i, c))
    out = pl.BlockSpec((t, nbb, e), lambda i: (0, i, 0))
    return pl.pallas_call(
        _sgu_sample_kernel,
        grid=(b // nbb,),
        in_specs=[col(0), col(1), col(2), _resident((1, e)), _resident(coef.shape), _resident(bias.shape)],
        out_specs=[out, out],
        out_shape=[jax.ShapeDtypeStruct((t, b, e), BF16), jax.ShapeDtypeStruct((t, b, e), F32)],
        compiler_params=_params("parallel"),
        name="sgu_sample",
    )(proj, proj, proj, g.reshape(1, e), coef, bias)


def _conv_sample_kernel(bg_ref, cg_ref, hv_ref, z_ref, w_ref, st_ref, o_ref, ns_ref):
    steps = cg_ref.shape[0]
    xp = [st_ref[k] for k in range(CONV_W - 1)] + [cg_ref[t] * hv_ref[t] for t in range(steps)]
    for t in range(steps):
        y = w_ref[0:1, :] * xp[t]
        for k in range(1, CONV_W):
            y = y + w_ref[k:k + 1, :] * xp[t + k]
        o_ref[t] = (bg_ref[t] * y * _silu(z_ref[t])).astype(BF16)
    for k in range(CONV_W - 1):
        ns_ref[k] = xp[steps + k]


def conv_sample(proj, w, state, *, e, nbb=32):
    t, b, _ = proj.shape
    col = lambda c: pl.BlockSpec((t, nbb, e), lambda i: (0, i, c))
    st = pl.BlockSpec((CONV_W - 1, nbb, e), lambda i: (0, i, 0))
    return pl.pallas_call(
        _conv_sample_kernel,
        grid=(b // nbb,),
        in_specs=[col(0), col(1), col(2), col(3), _resident((CONV_W, e)), st],
        out_specs=[pl.BlockSpec((t, nbb, e), lambda i: (0, i, 0)), st],
        out_shape=[jax.ShapeDtypeStruct((t, b, e), BF16), jax.ShapeDtypeStruct((CONV_W - 1, b, e), F32)],
        compiler_params=_params("parallel"),
        name="conv_sample",
    )(proj, proj, proj, proj, w, state)


LAYER_COLS = 256


def _attend_into(q, k_ref, v_ref, y_ref, col0):
    scale = XA_HEAD_DIM ** -0.5
    qb = q.astype(BF16)
    for h in range(XA_HEADS):
        hs = slice(h * XA_HEAD_DIM, (h + 1) * XA_HEAD_DIM)
        s = lax.dot_general(qb[:, hs], k_ref[:, hs], (((1,), (1,)), ((), ())),
                            preferred_element_type=F32) * scale
        ex = jnp.exp(s - jnp.max(s, axis=-1, keepdims=True))
        p = ex / jnp.sum(ex, axis=-1, keepdims=True)
        o = jnp.dot(p.astype(BF16), v_ref[:, hs], preferred_element_type=F32)
        y_ref[:, col0 + h * XA_HEAD_DIM:col0 + (h + 1) * XA_HEAD_DIM] = o.astype(BF16)


def _residual_out(x, y_ref, wo_ref, fg_ref, o_ref, final_norm):
    acc = x + jnp.dot(y_ref[...], wo_ref[...], preferred_element_type=F32)
    o_ref[...] = _rms(acc, fg_ref[...]) if final_norm else acc


def _sgu_layer_kernel(x_ref, ng_ref, w_ref, sg_ref, ws_ref, bias_ref, k_ref, v_ref, wo_ref, fg_ref, o_ref,
                      hn_ref, vn_ref, y_ref, *, e, final_norm):
    tm = x_ref.shape[0]
    gd = e // SGU_GROUPS
    x = x_ref[...]
    hn_ref[...] = _rms(x, ng_ref[...]).astype(BF16)
    hn = hn_ref[...]

    for c0 in range(0, e, LAYER_COLS):
        vn_ref[:, c0:c0 + LAYER_COLS] = jnp.dot(hn, w_ref[:, e + c0:e + c0 + LAYER_COLS],
                                                preferred_element_type=F32)
    vn_ref[...] = _rms(vn_ref[...], sg_ref[...])

    row = lax.broadcasted_iota(jnp.int32, (CHUNK, CHUNK), 0)
    col = lax.broadcasted_iota(jnp.int32, (CHUNK, CHUNK), 1)
    for g in range(SGU_GROUPS):
        cs = slice(g * gd, (g + 1) * gd)
        wg = jnp.where(row >= col, ws_ref[g], 0.0).astype(BF16)
        u = jnp.dot(hn, w_ref[:, g * gd:(g + 1) * gd], preferred_element_type=F32)
        z = jnp.dot(hn, w_ref[:, 2 * e + g * gd:2 * e + (g + 1) * gd], preferred_element_type=F32)
        for c in range(tm // CHUNK):
            rs = slice(c * CHUNK, (c + 1) * CHUNK)
            mixed = jnp.dot(wg, vn_ref[rs, cs].astype(BF16), preferred_element_type=F32) + bias_ref[:, cs]
            y_ref[rs, cs] = (u[rs] * mixed * _silu(z[rs])).astype(BF16)

    q = jnp.dot(hn, w_ref[:, 3 * e:3 * e + XA_DIM], preferred_element_type=F32)
    _attend_into(q, k_ref, v_ref, y_ref, e)
    _residual_out(x, y_ref, wo_ref, fg_ref, o_ref, final_norm)


def sgu_layer(x, ng, w, sg, ws, bias, kb, vb, layer, wo, fg, *, b, e, final_norm, tm=256):
    m, d = x.shape
    lt = m // b // tm
    mem = kb.shape[2]
    kv = pl.BlockSpec((None, None, mem, XA_DIM), lambda bi, i: (layer, bi, 0, 0))
    return pl.pallas_call(
        functools.partial(_sgu_layer_kernel, e=e, final_norm=final_norm),
        grid=(b, lt),
        in_specs=[pl.BlockSpec((tm, d), lambda bi, i: (bi * lt + i, 0)),
                  _resident((1, d)), _resident(w.shape), _resident((1, e)), _resident(ws.shape),
                  _resident(bias.shape), kv, kv, _resident(wo.shape), _resident((1, d))],
        out_specs=pl.BlockSpec((tm, d), lambda bi, i: (bi * lt + i, 0)),
        out_shape=jax.ShapeDtypeStruct((m, d), F32),
        scratch_shapes=[pltpu.VMEM((tm, d), BF16), pltpu.VMEM((tm, e), F32),
                        pltpu.VMEM((tm, e + XA_DIM), BF16)],
        compiler_params=_params("parallel", "arbitrary"),
        name="sgu_layer",
    )(x, ng.reshape(1, d), w, sg.reshape(1, e), ws, bias, kb, vb, wo, fg.reshape(1, d))


def _conv_layer_kernel(x_ref, ng_ref, w_ref, cw_ref, init_ref, k_ref, v_ref, wo_ref, fg_ref, o_ref, st_ref,
                       hn_ref, y_ref, carry_ref, *, e, final_norm):
    tm = x_ref.shape[0]
    keep = CONV_W - 1
    edge = carry_ref.shape[0]

    @pl.when(pl.program_id(1) == 0)
    def _():
        carry_ref[...] = jnp.zeros_like(carry_ref)
        carry_ref[edge - keep:edge, :] = init_ref[...]

    x = x_ref[...]
    hn_ref[...] = _rms(x, ng_ref[...]).astype(BF16)
    hn = hn_ref[...]
    rid = lax.broadcasted_iota(jnp.int32, (tm, LAYER_COLS), 0)
    for c0 in range(0, e, LAYER_COLS):
        cs = slice(c0, c0 + LAYER_COLS)
        part = lambda k: jnp.dot(hn, w_ref[:, k * e + c0:k * e + c0 + LAYER_COLS], preferred_element_type=F32)
        xc = part(1) * part(2)
        p1 = carry_ref[edge - 1:edge, cs]
        p2 = carry_ref[edge - 2:edge - 1, cs]
        x1 = jnp.where(rid == 0, p1, pltpu.roll(xc, 1, 0))
        x2 = jnp.where(rid == 0, p2, jnp.where(rid == 1, p1, pltpu.roll(xc, 2, 0)))
        y = cw_ref[0:1, cs] * x2 + cw_ref[1:2, cs] * x1 + cw_ref[2:3, cs] * xc
        y_ref[:, cs] = (part(0) * y * _silu(part(3))).astype(BF16)
        carry_ref[:, cs] = xc[tm - edge:, :]
        st_ref[:, cs] = xc[tm - keep:, :]

    q = jnp.dot(hn, w_ref[:, 4 * e:4 * e + XA_DIM], preferred_element_type=F32)
    _attend_into(q, k_ref, v_ref, y_ref, e)
    _residual_out(x, y_ref, wo_ref, fg_ref, o_ref, final_norm)


def conv_layer(x, ng, w, cw, init, kb, vb, layer, wo, fg, *, b, e, final_norm, tm=256):
    m, d = x.shape
    lt = m // b // tm
    mem = kb.shape[2]
    kv = pl.BlockSpec((None, None, mem, XA_DIM), lambda bi, i: (layer, bi, 0, 0))
    st = pl.BlockSpec((None, CONV_W - 1, e), lambda bi, i: (bi, 0, 0))
    return pl.pallas_call(
        functools.partial(_conv_layer_kernel, e=e, final_norm=final_norm),
        grid=(b, lt),
        in_specs=[pl.BlockSpec((tm, d), lambda bi, i: (bi * lt + i, 0)),
                  _resident((1, d)), _resident(w.shape), _resident((CONV_W, e)), st,
                  kv, kv, _resident(wo.shape), _resident((1, d))],
        out_specs=[pl.BlockSpec((tm, d), lambda bi, i: (bi * lt + i, 0)), st],
        out_shape=[jax.ShapeDtypeStruct((m, d), F32), jax.ShapeDtypeStruct((b, CONV_W - 1, e), F32)],
        scratch_shapes=[pltpu.VMEM((tm, d), BF16), pltpu.VMEM((tm, e + XA_DIM), BF16),
                        pltpu.VMEM((V7X_SUBLANES, e), F32)],
        compiler_params=_params("parallel", "arbitrary"),
        name="conv_layer",
    )(x, ng.reshape(1, d), w, cw, init, kb, vb, wo, fg.reshape(1, d))


def _s5_disc_kernel(lre_ref, lim_ref, ldt_ref, bre_ref, bim_ref, ar_ref, ai_ref, bbr_ref, bbi_ref):
    dt = jnp.exp(ldt_ref[...])
    lr, li = lre_ref[...], lim_ref[...]
    mag = jnp.exp(lr * dt)
    ar = mag * jnp.cos(li * dt)
    ai = mag * jnp.sin(li * dt)
    nr, ni = ar - 1, ai
    den = lr * lr + li * li
    cr = (nr * lr + ni * li) / den
    ci = (ni * lr - nr * li) / den
    ar_ref[...] = ar
    ai_ref[...] = ai
    cr3, ci3 = cr[:, None, :], ci[:, None, :]
    bbr_ref[...] = cr3 * bre_ref[...] - ci3 * bim_ref[...]
    bbi_ref[...] = cr3 * bim_ref[...] + ci3 * bre_ref[...]


def s5_discretize(lam_re, lam_im, log_dt, b_re, b_im):
    g, p = lam_re.shape
    gp = jax.ShapeDtypeStruct((g, p), F32)
    gip = jax.ShapeDtypeStruct(b_re.shape, F32)
    return pl.pallas_call(
        _s5_disc_kernel,
        out_shape=[gp, gp, gip, gip],
        compiler_params=pltpu.CompilerParams(vmem_limit_bytes=V7X_VMEM_LIMIT_BYTES),
        name="s5_discretize",
    )(lam_re, lam_im, log_dt.reshape(g, 1), b_re, b_im)


def _s5_kernel(u_ref, bre_ref, bim_ref, cre_ref, cim_ref, ar_ref, ai_ref, d_ref, h0r_ref, h0i_ref,
               g_ref, hr_ref, hi_ref, sr_ref, si_ref):
    tl, nb, e = u_ref.shape
    rows = tl * nb
    n_blk = bre_ref.shape[0]
    fk = bre_ref.shape[1]
    sk = bre_ref.shape[2]
    lanes = sr_ref.shape[1]

    @pl.when(pl.program_id(1) == 0)
    def _():
        hr_ref[...] = h0r_ref[...]
        hi_ref[...] = h0i_ref[...]

    u = u_ref[...].reshape(rows, e)
    ub = u.astype(BF16)
    for kb in range(n_blk):
        uk = ub[:, kb * fk:(kb + 1) * fk]
        sr_ref[:, kb * sk:(kb + 1) * sk] = jnp.dot(uk, bre_ref[kb], preferred_element_type=F32)
        si_ref[:, kb * sk:(kb + 1) * sk] = jnp.dot(uk, bim_ref[kb], preferred_element_type=F32)

    for c in range(lanes // S5_SCAN_LANES):
        ls = slice(c * S5_SCAN_LANES, (c + 1) * S5_SCAN_LANES)
        a_r = jnp.broadcast_to(ar_ref[:, ls], (V7X_SUBLANES, S5_SCAN_LANES))
        a_i = jnp.broadcast_to(ai_ref[:, ls], (V7X_SUBLANES, S5_SCAN_LANES))

        def sweep(sg, _):
            r0 = pl.multiple_of(sg * V7X_SUBLANES, V7X_SUBLANES)

            def step(t, h):
                hr, hi = h
                row = pl.multiple_of(t * nb + r0, V7X_SUBLANES)
                nr = a_r * hr - a_i * hi + sr_ref[pl.ds(row, V7X_SUBLANES), ls]
                ni = a_r * hi + a_i * hr + si_ref[pl.ds(row, V7X_SUBLANES), ls]
                sr_ref[pl.ds(row, V7X_SUBLANES), ls] = nr
                si_ref[pl.ds(row, V7X_SUBLANES), ls] = ni
                return nr, ni

            h0 = (hr_ref[pl.ds(r0, V7X_SUBLANES), ls], hi_ref[pl.ds(r0, V7X_SUBLANES), ls])
            hr, hi = lax.fori_loop(0, tl, step, h0, unroll=min(tl, 8))
            hr_ref[pl.ds(r0, V7X_SUBLANES), ls] = hr
            hi_ref[pl.ds(r0, V7X_SUBLANES), ls] = hi
            return 0

        lax.fori_loop(0, nb // V7X_SUBLANES, sweep, 0)

    for kb in range(n_blk):
        hrb = sr_ref[:, kb * sk:(kb + 1) * sk].astype(BF16)
        hib = si_ref[:, kb * sk:(kb + 1) * sk].astype(BF16)
        y = jnp.dot(hrb, cre_ref[kb], preferred_element_type=F32)
        y = y - jnp.dot(hib, cim_ref[kb], preferred_element_type=F32)
        fs = slice(kb * fk, (kb + 1) * fk)
        y = y + d_ref[:, fs] * u[:, fs]
        g_ref[:, :, fs] = jax.nn.gelu(y).reshape(tl, nb, fk)


def s5_scan(proj, bre, bim, cre, cim, ar, ai, d, h0r, h0i, *, e, tl, nbb):
    l, b, _ = proj.shape
    lanes = h0r.shape[1]
    rows = tl * nbb
    st = pl.BlockSpec((nbb, lanes), lambda bi, i: (bi, 0))
    return pl.pallas_call(
        _s5_kernel,
        grid=(b // nbb, l // tl),
        in_specs=[pl.BlockSpec((tl, nbb, e), lambda bi, i: (i, bi, 0)),
                  _resident(bre.shape), _resident(bim.shape), _resident(cre.shape), _resident(cim.shape),
                  _resident((1, lanes)), _resident((1, lanes)), _resident((1, e)), st, st],
        out_specs=[pl.BlockSpec((tl, nbb, e), lambda bi, i: (i, bi, 0)), st, st],
        out_shape=[jax.ShapeDtypeStruct((l, b, e), F32),
                   jax.ShapeDtypeStruct((b, lanes), F32), jax.ShapeDtypeStruct((b, lanes), F32)],
        scratch_shapes=[pltpu.VMEM((rows, lanes), F32), pltpu.VMEM((rows, lanes), F32)],
        compiler_params=_params("parallel", "arbitrary"),
        name="s5_scan",
    )(proj, bre, bim, cre, cim, ar.reshape(1, lanes), ai.reshape(1, lanes), d.reshape(1, e), h0r, h0i)


GLU_COLS = 512


def _glu_kernel(g_ref, w_ref, b_ref, z_ref, o_ref, gb_ref):
    gb_ref[...] = g_ref[...].astype(BF16)
    for c0 in range(0, o_ref.shape[1], GLU_COLS):
        cs = slice(c0, c0 + GLU_COLS)
        a = jnp.dot(gb_ref[...], w_ref[:, cs], preferred_element_type=F32) + b_ref[:, cs]
        o_ref[:, cs] = (g_ref[:, cs] * jax.nn.sigmoid(a) * _silu(z_ref[:, cs])).astype(BF16)


def glu_gate(g, w, bias, proj, *, e, tm=512):
    m = g.shape[0]
    tm = min(tm, m)
    return pl.pallas_call(
        _glu_kernel,
        grid=(m // tm,),
        in_specs=[pl.BlockSpec((tm, e), lambda i: (i, 0)),
                  _resident((e, e)),
                  _resident((1, e)),
                  pl.BlockSpec((tm, e), lambda i: (i, 1))],
        out_specs=pl.BlockSpec((tm, e), lambda i: (i, 0)),
        out_shape=jax.ShapeDtypeStruct((m, e), BF16),
        scratch_shapes=[pltpu.VMEM((tm, e), BF16)],
        compiler_params=_params("parallel"),
        name="glu_gate",
    )(g, w, bias.reshape(1, e), proj)


def _block_diag(x, n_blk):
    g, r, c = x.shape
    gb = g // n_blk
    eye = jnp.eye(gb, dtype=x.dtype)
    out = jnp.einsum("bgrc,gh->bgrhc", x.reshape(n_blk, gb, r, c), eye)
    return out.reshape(n_blk, gb * r, gb * c)


def kernel(x_prompt, x_sample, mem_prompt, cache_mem_k, cache_mem_v, state_conv, state_s5_re, state_s5_im,
           norm_g, final_g, mem_norm_g, w_kv, w_out, w_in_a, sgu_norm_g, sgu_w, sgu_b, w_in_b, conv_w, w_in_c,
           s5_lam_re, s5_lam_im, s5_log_dt, s5_b_re, s5_b_im, s5_c_re, s5_c_im, s5_d, w_glu, b_glu):
    bp, lp, d = x_prompt.shape
    bs, ls, _ = x_sample.shape
    depth = norm_g.shape[0]
    e = conv_w.shape[-1]
    mem_len = mem_prompt.shape[1]
    n_grp, n_state = s5_lam_re.shape[1:]
    gd = e // SGU_GROUPS
    n_blk = e // S5_FEATS_PER_BLOCK
    kinds = [i % N_MIXERS for i in range(depth)]
    kidx = [sum(1 for j in range(i) if kinds[j] == kinds[i]) for i in range(depth)]

    mk, mv, mkb, mvb = mem_kv(mem_prompt.reshape(bp * mem_len, d), mem_norm_g, w_kv.astype(BF16))
    mem_k_prompt = mk.reshape(depth, bp, mem_len, XA_HEADS, XA_HEAD_DIM)
    mem_v_prompt = mv.reshape(depth, bp, mem_len, XA_HEADS, XA_HEAD_DIM)
    mkb = mkb.reshape(depth, bp, mem_len, XA_DIM)
    mvb = mvb.reshape(depth, bp, mem_len, XA_DIM)
    ck = cache_mem_k.reshape(depth, bs, mem_len * XA_HEADS, XA_HEAD_DIM)
    cv = cache_mem_v.reshape(depth, bs, mem_len * XA_HEADS, XA_HEAD_DIM)
    q_head = jnp.arange(XA_HEADS * ls, dtype=jnp.int32)[:, None] // ls
    kv_head = jnp.arange(mem_len * XA_HEADS, dtype=jnp.int32)[None, :] % XA_HEADS
    head_bias = jnp.where(q_head == kv_head, 0.0, -1e30).astype(F32)

    xp = x_prompt.reshape(bp * lp, d)
    xs = x_sample.transpose(1, 0, 2).reshape(ls * bs, d)

    def sample_attention(proj, qcol, i):
        q = proj.reshape(ls, bs, -1)[:, :, qcol * XA_DIM:(qcol + 1) * XA_DIM]
        q = q.reshape(ls, bs, XA_HEADS, XA_HEAD_DIM).transpose(1, 2, 0, 3).reshape(bs, XA_HEADS * ls, XA_HEAD_DIM)
        xa = cross_attention_sample(q, ck, cv, head_bias, i)
        xa = xa.reshape(bs, XA_HEADS, ls, XA_HEAD_DIM).transpose(2, 0, 1, 3)
        return xa.reshape(ls * bs, XA_DIM).astype(BF16)

    conv_p, conv_s, s5r_p, s5i_p, s5r_s, s5i_s, chunk_v = [], [], [], [], [], [], []
    for i in range(depth):
        kind, j = kinds[i], kidx[i]
        wo = w_out[i].astype(BF16)
        w1, w2 = wo[:e], wo[e:]
        last = i == depth - 1
        if kind == 0:
            w = w_in_a[j].astype(BF16)
            qcol = 3 * e // XA_DIM
            bias = jnp.repeat(sgu_b[j].T, gd, axis=1)
            xp = sgu_layer(xp, norm_g[i], w, sgu_norm_g[j], sgu_w[j], bias, mkb, mvb, i, wo, final_g,
                           b=bp, e=e, final_norm=last)
            ps = norm_proj(xs, norm_g[i], w, tm=512)
            coef = jnp.repeat(sgu_w[j][:, :ls, :ls].transpose(1, 2, 0).reshape(ls * ls, SGU_GROUPS), gd, axis=1)
            ys, vn = sgu_sample(ps.reshape(ls, bs, -1), sgu_norm_g[j], coef, bias[:V7X_SUBLANES], e=e)
            chunk_v.append(vn.transpose(1, 0, 2))
            ys = ys.reshape(ls * bs, e)
            as_ = sample_attention(ps, qcol, i)
        elif kind == 1:
            w = w_in_b[j].astype(BF16)
            qcol = 4 * e // XA_DIM
            xp, cst = conv_layer(xp, norm_g[i], w, conv_w[j], jnp.zeros((bp, CONV_W - 1, e), F32), mkb, mvb, i,
                                 wo, final_g, b=bp, e=e, final_norm=last)
            conv_p.append(cst)
            ps = norm_proj(xs, norm_g[i], w, tm=512)
            ys, nst = conv_sample(ps.reshape(ls, bs, -1), conv_w[j], state_conv[j].transpose(1, 0, 2), e=e)
            conv_s.append(nst.transpose(1, 0, 2))
            ys = ys.reshape(ls * bs, e)
            as_ = sample_attention(ps, qcol, i)
        else:
            wc = w_in_c[j].astype(BF16)
            w_uz, w_q = wc[:, :2 * e], wc[:, 2 * e:]
            ar, ai, bbr, bbi = s5_discretize(s5_lam_re[j], s5_lam_im[j], s5_log_dt[j],
                                             s5_b_re[j].transpose(0, 2, 1), s5_b_im[j].transpose(0, 2, 1))
            bre = _block_diag(bbr, n_blk).astype(BF16)
            bim = _block_diag(bbi, n_blk).astype(BF16)
            cre = _block_diag(s5_c_re[j].transpose(0, 2, 1), n_blk).astype(BF16)
            cim = _block_diag(s5_c_im[j].transpose(0, 2, 1), n_blk).astype(BF16)
            ar, ai = ar.reshape(-1), ai.reshape(-1)

            xt = xp.reshape(bp, lp, d).transpose(1, 0, 2).reshape(lp * bp, d)
            pt = norm_proj(xt, norm_g[i], w_uz, tm=2048)
            pq = norm_proj(xp, norm_g[i], w_q, tm=2048)
            zero = jnp.zeros((bp, n_grp * n_state), F32)
            gp, hr, hi = s5_scan(pt.reshape(lp, bp, -1), bre, bim, cre, cim, ar, ai, s5_d[j], zero, zero,
                                 e=e, tl=32, nbb=bp)
            s5r_p.append(hr.reshape(bp, n_grp, n_state))
            s5i_p.append(hi.reshape(bp, n_grp, n_state))
            yt = glu_gate(gp.reshape(lp * bp, e), w_glu[j].astype(BF16), b_glu[j], pt, e=e)
            ap = cross_attention(pq.reshape(bp, lp, -1), 0, mkb[i], mvb[i], nb=1, tq=512, out_dtype=BF16)
            at = ap.transpose(1, 0, 2).reshape(lp * bp, XA_DIM)
            xt = out_proj(yt, at, w1, w2, xt, final_g, final_norm=last)
            xp = xt.reshape(lp, bp, d).transpose(1, 0, 2).reshape(bp * lp, d)

            ps = norm_proj(xs, norm_g[i], wc, tm=512)
            gs, hr, hi = s5_scan(ps.reshape(ls, bs, -1), bre, bim, cre, cim, ar, ai, s5_d[j],
                                 state_s5_re[j].reshape(bs, -1), state_s5_im[j].reshape(bs, -1),
                                 e=e, tl=ls, nbb=32)
            s5r_s.append(hr.reshape(bs, n_grp, n_state))
            s5i_s.append(hi.reshape(bs, n_grp, n_state))
            ys = glu_gate(gs.reshape(ls * bs, e), w_glu[j].astype(BF16), b_glu[j], ps, e=e)
            as_ = sample_attention(ps, 2 * e // XA_DIM, i)
        xs = out_proj(ys, as_, w1, w2, xs, final_g, final_norm=last)

    y_prompt = xp.reshape(bp, lp, d)
    y_sample = xs.reshape(ls, bs, d).transpose(1, 0, 2)
    return (y_prompt, y_sample, mem_k_prompt, mem_v_prompt, jnp.stack(conv_p), jnp.stack(conv_s),
            jnp.stack(s5r_p), jnp.stack(s5i_p), jnp.stack(s5r_s), jnp.stack(s5i_s), jnp.stack(chunk_v))
```

```python
import functools

import jax
import jax.numpy as jnp
from jax import lax
from jax.experimental import pallas as pl
from jax.experimental.pallas import tpu as pltpu

F32 = jnp.float32
BF16 = jnp.bfloat16

EPS = 1e-6
N_MIXERS = 3
CHUNK = 128
SGU_GROUPS = 8
CONV_W = 3
S5_IN = 16
XA_HEADS = 4
XA_HEAD_DIM = 128
XA_DIM = XA_HEADS * XA_HEAD_DIM

V7X_SUBLANES = 8
V7X_VMEM_LIMIT_BYTES = 56 * 1024 * 1024
S5_FEATS_PER_BLOCK = 256
S5_SCAN_LANES = 512


def _params(*semantics):
    return pltpu.CompilerParams(dimension_semantics=semantics, vmem_limit_bytes=V7X_VMEM_LIMIT_BYTES)


def _resident(shape):
    nd = len(shape)
    return pl.BlockSpec(shape, lambda *_: (0,) * nd, pipeline_mode=pl.Buffered(1))


def _rms(x, g):
    return x * lax.rsqrt(jnp.mean(x * x, axis=-1, keepdims=True) + EPS) * g


def _silu(z):
    return z * jax.nn.sigmoid(z)


def _norm_proj_kernel(x_ref, g_ref, w_ref, o_ref, hn_ref):
    @pl.when(pl.program_id(1) == 0)
    def _():
        hn_ref[...] = _rms(x_ref[...], g_ref[...]).astype(BF16)

    o_ref[...] = jnp.dot(hn_ref[...], w_ref[...], preferred_element_type=F32)


def norm_proj(x, g, w, *, tm, tn=512):
    m, d = x.shape
    n = w.shape[1]
    tm = min(tm, m)
    return pl.pallas_call(
        _norm_proj_kernel,
        grid=(m // tm, n // tn),
        in_specs=[pl.BlockSpec((tm, d), lambda i, j: (i, 0)),
                  pl.BlockSpec((1, d), lambda i, j: (0, 0)),
                  pl.BlockSpec((d, tn), lambda i, j: (0, j))],
        out_specs=pl.BlockSpec((tm, tn), lambda i, j: (i, j)),
        out_shape=jax.ShapeDtypeStruct((m, n), F32),
        scratch_shapes=[pltpu.VMEM((tm, d), BF16)],
        compiler_params=_params("parallel", "arbitrary"),
        name="norm_proj",
    )(x, g.reshape(1, d), w)


def _mem_kv_kernel(x_ref, g_ref, w_ref, k_ref, v_ref, kb_ref, vb_ref):
    hn = _rms(x_ref[...], g_ref[0]).astype(BF16)
    kv = jnp.dot(hn, w_ref[0], preferred_element_type=F32)
    k, v = kv[:, :XA_DIM], kv[:, XA_DIM:]
    k_ref[0] = k
    v_ref[0] = v
    kb_ref[0] = k.astype(BF16)
    vb_ref[0] = v.astype(BF16)


def mem_kv(mem, g, w, *, tm=512):
    m, d = mem.shape
    depth = g.shape[0]
    out = pl.BlockSpec((1, tm, XA_DIM), lambda l, i: (l, i, 0))
    shape = lambda dt: jax.ShapeDtypeStruct((depth, m, XA_DIM), dt)
    return pl.pallas_call(
        _mem_kv_kernel,
        grid=(depth, m // tm),
        in_specs=[pl.BlockSpec((tm, d), lambda l, i: (i, 0)),
                  pl.BlockSpec((1, 1, d), lambda l, i: (l, 0, 0)),
                  pl.BlockSpec((1, d, 2 * XA_DIM), lambda l, i: (l, 0, 0))],
        out_specs=[out, out, out, out],
        out_shape=[shape(F32), shape(F32), shape(BF16), shape(BF16)],
        compiler_params=_params("parallel", "parallel"),
        name="mem_kv",
    )(mem, g.reshape(depth, 1, d), w)


def _attn_kernel(q_ref, k_ref, v_ref, o_ref, *, nb):
    scale = XA_HEAD_DIM ** -0.5
    for b in range(nb):
        q = q_ref[b].astype(BF16)
        k = k_ref[b].astype(BF16)
        v = v_ref[b].astype(BF16)
        for h in range(XA_HEADS):
            hs = slice(h * XA_HEAD_DIM, (h + 1) * XA_HEAD_DIM)
            s = lax.dot_general(q[:, hs], k[:, hs], (((1,), (1,)), ((), ())),
                                preferred_element_type=F32) * scale
            e = jnp.exp(s - jnp.max(s, axis=-1, keepdims=True))
            p = e / jnp.sum(e, axis=-1, keepdims=True)
            o = jnp.dot(p.astype(BF16), v[:, hs], preferred_element_type=F32)
            o_ref[b, :, hs] = o.astype(o_ref.dtype)


def cross_attention(q, qcol, k, v, *, nb, tq, out_dtype):
    b, l, _ = q.shape
    mem = k.shape[1]
    return pl.pallas_call(
        functools.partial(_attn_kernel, nb=nb),
        grid=(b // nb, l // tq),
        in_specs=[pl.BlockSpec((nb, tq, XA_DIM), lambda i, j: (i, j, qcol)),
                  pl.BlockSpec((nb, mem, XA_DIM), lambda i, j: (i, 0, 0)),
                  pl.BlockSpec((nb, mem, XA_DIM), lambda i, j: (i, 0, 0))],
        out_specs=pl.BlockSpec((nb, tq, XA_DIM), lambda i, j: (i, j, 0)),
        out_shape=jax.ShapeDtypeStruct((b, l, XA_DIM), out_dtype),
        compiler_params=_params("parallel", "arbitrary"),
        name="cross_attention",
    )(q, k, v)


def _attn_sample_kernel(q_ref, k_ref, v_ref, bias_ref, o_ref, *, nb):
    scale = XA_HEAD_DIM ** -0.5
    for b in range(nb):
        q = q_ref[b].astype(BF16)
        k = k_ref[b].astype(BF16)
        v = v_ref[b].astype(BF16)
        s = lax.dot_general(q, k, (((1,), (1,)), ((), ())), preferred_element_type=F32) * scale + bias_ref[...]
        e = jnp.exp(s - jnp.max(s, axis=-1, keepdims=True))
        p = e / jnp.sum(e, axis=-1, keepdims=True)
        o_ref[b] = jnp.dot(p.astype(BF16), v, preferred_element_type=F32)


def cross_attention_sample(q, cache_k, cache_v, bias, layer, *, nb=8):
    b, nq, dh = q.shape
    nkv = cache_k.shape[2]
    kv = pl.BlockSpec((None, nb, nkv, dh), lambda i: (layer, i, 0, 0))
    return pl.pallas_call(
        functools.partial(_attn_sample_kernel, nb=nb),
        grid=(b // nb,),
        in_specs=[pl.BlockSpec((nb, nq, dh), lambda i: (i, 0, 0)), kv, kv, _resident((nq, nkv))],
        out_specs=pl.BlockSpec((nb, nq, dh), lambda i: (i, 0, 0)),
        out_shape=jax.ShapeDtypeStruct((b, nq, dh), F32),
        compiler_params=_params("parallel"),
        name="cross_attention_sample",
    )(q, cache_k, cache_v, bias)


def _out_proj_kernel(y_ref, xa_ref, w1_ref, w2_ref, x_ref, fg_ref, o_ref, *, final_norm):
    acc = x_ref[...] + jnp.dot(y_ref[...], w1_ref[...], preferred_element_type=F32)
    acc = acc + jnp.dot(xa_ref[...], w2_ref[...], preferred_element_type=F32)
    o_ref[...] = _rms(acc, fg_ref[...]) if final_norm else acc


def out_proj(y, xa, w1, w2, x, final_g, *, final_norm, tm=512):
    m, d = x.shape
    e, a = y.shape[1], xa.shape[1]
    tm = min(tm, m)
    return pl.pallas_call(
        functools.partial(_out_proj_kernel, final_norm=final_norm),
        grid=(m // tm,),
        in_specs=[pl.BlockSpec((tm, e), lambda i: (i, 0)),
                  pl.BlockSpec((tm, a), lambda i: (i, 0)),
                  _resident((e, d)),
                  _resident((a, d)),
                  pl.BlockSpec((tm, d), lambda i: (i, 0)),
                  _resident((1, d))],
        out_specs=pl.BlockSpec((tm, d), lambda i: (i, 0)),
        out_shape=jax.ShapeDtypeStruct((m, d), F32),
        compiler_params=_params("parallel"),
        name="out_proj",
    )(y, xa, w1, w2, x, final_g.reshape(1, d))


def _sgu_sample_kernel(u_ref, v_ref, z_ref, g_ref, coef_ref, bias_ref, o_ref, vn_ref):
    steps = u_ref.shape[0]
    vn = []
    for t in range(steps):
        n = _rms(v_ref[t], g_ref[...])
        vn_ref[t] = n
        vn.append(n)
    for t in range(steps):
        mixed = bias_ref[t:t + 1, :]
        for s in range(t + 1):
            mixed = mixed + coef_ref[t * steps + s:t * steps + s + 1, :] * vn[s]
        o_ref[t] = (u_ref[t] * mixed * _silu(z_ref[t])).astype(BF16)


def sgu_sample(proj, g, coef, bias, *, e, nbb=32):
    t, b, _ = proj.shape
    col = lambda c: pl.BlockSpec((t, nbb, e), lambda i: (0, i, c))
    out = pl.BlockSpec((t, nbb, e), lambda i: (0, i, 0))
    return pl.pallas_call(
        _sgu_sample_kernel,
        grid=(b // nbb,),
        in_specs=[col(0), col(1), col(2), _resident((1, e)), _resident(coef.shape), _resident(bias.shape)],
        out_specs=[out, out],
        out_shape=[jax.ShapeDtypeStruct((t, b, e), BF16), jax.ShapeDtypeStruct((t, b, e), F32)],
        compiler_params=_params("parallel"),
        name="sgu_sample",
    )(proj, proj, proj, g.reshape(1, e), coef, bias)


def _conv_sample_kernel(bg_ref, cg_ref, hv_ref, z_ref, w_ref, st_ref, o_ref, ns_ref):
    steps = cg_ref.shape[0]
    xp = [st_ref[k] for k in range(CONV_W - 1)] + [cg_ref[t] * hv_ref[t] for t in range(steps)]
    for t in range(steps):
        y = w_ref[0:1, :] * xp[t]
        for k in range(1, CONV_W):
            y = y + w_ref[k:k + 1, :] * xp[t + k]
        o_ref[t] = (bg_ref[t] * y * _silu(z_ref[t])).astype(BF16)
    for k in range(CONV_W - 1):
        ns_ref[k] = xp[steps + k]


def conv_sample(proj, w, state, *, e, nbb=32):
    t, b, _ = proj.shape
    col = lambda c: pl.BlockSpec((t, nbb, e), lambda i: (0, i, c))
    st = pl.BlockSpec((CONV_W - 1, nbb, e), lambda i: (0, i, 0))
    return pl.pallas_call(
        _conv_sample_kernel,
        grid=(b // nbb,),
        in_specs=[col(0), col(1), col(2), col(3), _resident((CONV_W, e)), st],
        out_specs=[pl.BlockSpec((t, nbb, e), lambda i: (0, i, 0)), st],
        out_shape=[jax.ShapeDtypeStruct((t, b, e), BF16), jax.ShapeDtypeStruct((CONV_W - 1, b, e), F32)],
        compiler_params=_params("parallel"),
        name="conv_sample",
    )(proj, proj, proj, proj, w, state)


LAYER_COLS = 256


def _attend_into(q, k_ref, v_ref, y_ref, col0):
    scale = XA_HEAD_DIM ** -0.5
    qb = q.astype(BF16)
    for h in range(XA_HEADS):
        hs = slice(h * XA_HEAD_DIM, (h + 1) * XA_HEAD_DIM)
        s = lax.dot_general(qb[:, hs], k_ref[:, hs], (((1,), (1,)), ((), ())),
                            preferred_element_type=F32) * scale
        ex = jnp.exp(s - jnp.max(s, axis=-1, keepdims=True))
        p = ex / jnp.sum(ex, axis=-1, keepdims=True)
        o = jnp.dot(p.astype(BF16), v_ref[:, hs], preferred_element_type=F32)
        y_ref[:, col0 + h * XA_HEAD_DIM:col0 + (h + 1) * XA_HEAD_DIM] = o.astype(BF16)


def _residual_out(x, y_ref, wo_ref, fg_ref, o_ref, final_norm):
    acc = x + jnp.dot(y_ref[...], wo_ref[...], preferred_element_type=F32)
    o_ref[...] = _rms(acc, fg_ref[...]) if final_norm else acc


def _sgu_layer_kernel(x_ref, ng_ref, w_ref, sg_ref, ws_ref, bias_ref, k_ref, v_ref, wo_ref, fg_ref, o_ref,
                      hn_ref, vn_ref, y_ref, *, e, final_norm):
    tm = x_ref.shape[0]
    gd = e // SGU_GROUPS
    x = x_ref[...]
    hn_ref[...] = _rms(x, ng_ref[...]).astype(BF16)
    hn = hn_ref[...]

    for c0 in range(0, e, LAYER_COLS):
        vn_ref[:, c0:c0 + LAYER_COLS] = jnp.dot(hn, w_ref[:, e + c0:e + c0 + LAYER_COLS],
                                                preferred_element_type=F32)
    vn_ref[...] = _rms(vn_ref[...], sg_ref[...])

    row = lax.broadcasted_iota(jnp.int32, (CHUNK, CHUNK), 0)
    col = lax.broadcasted_iota(jnp.int32, (CHUNK, CHUNK), 1)
    for g in range(SGU_GROUPS):
        cs = slice(g * gd, (g + 1) * gd)
        wg = jnp.where(row >= col, ws_ref[g], 0.0).astype(BF16)
        u = jnp.dot(hn, w_ref[:, g * gd:(g + 1) * gd], preferred_element_type=F32)
        z = jnp.dot(hn, w_ref[:, 2 * e + g * gd:2 * e + (g + 1) * gd], preferred_element_type=F32)
        for c in range(tm // CHUNK):
            rs = slice(c * CHUNK, (c + 1) * CHUNK)
            mixed = jnp.dot(wg, vn_ref[rs, cs].astype(BF16), preferred_element_type=F32) + bias_ref[:, cs]
            y_ref[rs, cs] = (u[rs] * mixed * _silu(z[rs])).astype(BF16)

    q = jnp.dot(hn, w_ref[:, 3 * e:3 * e + XA_DIM], preferred_element_type=F32)
    _attend_into(q, k_ref, v_ref, y_ref, e)
    _residual_out(x, y_ref, wo_ref, fg_ref, o_ref, final_norm)


def sgu_layer(x, ng, w, sg, ws, bias, kb, vb, layer, wo, fg, *, b, e, final_norm, tm=512):
    m, d = x.shape
    lt = m // b // tm
    mem = kb.shape[2]
    kv = pl.BlockSpec((None, None, mem, XA_DIM), lambda bi, i: (layer, bi, 0, 0))
    return pl.pallas_call(
        functools.partial(_sgu_layer_kernel, e=e, final_norm=final_norm),
        grid=(b, lt),
        in_specs=[pl.BlockSpec((tm, d), lambda bi, i: (bi * lt + i, 0)),
                  _resident((1, d)), _resident(w.shape), _resident((1, e)), _resident(ws.shape),
                  _resident(bias.shape), kv, kv, _resident(wo.shape), _resident((1, d))],
        out_specs=pl.BlockSpec((tm, d), lambda bi, i: (bi * lt + i, 0)),
        out_shape=jax.ShapeDtypeStruct((m, d), F32),
        scratch_shapes=[pltpu.VMEM((tm, d), BF16), pltpu.VMEM((tm, e), F32),
                        pltpu.VMEM((tm, e + XA_DIM), BF16)],
        compiler_params=_params("parallel", "arbitrary"),
        name="sgu_layer",
    )(x, ng.reshape(1, d), w, sg.reshape(1, e), ws, bias, kb, vb, wo, fg.reshape(1, d))


def _conv_layer_kernel(x_ref, ng_ref, w_ref, cw_ref, init_ref, k_ref, v_ref, wo_ref, fg_ref, o_ref, st_ref,
                       hn_ref, y_ref, carry_ref, *, e, final_norm):
    tm = x_ref.shape[0]
    keep = CONV_W - 1
    edge = carry_ref.shape[0]

    @pl.when(pl.program_id(1) == 0)
    def _():
        carry_ref[...] = jnp.zeros_like(carry_ref)
        carry_ref[edge - keep:edge, :] = init_ref[...]

    x = x_ref[...]
    hn_ref[...] = _rms(x, ng_ref[...]).astype(BF16)
    hn = hn_ref[...]
    rid = lax.broadcasted_iota(jnp.int32, (tm, LAYER_COLS), 0)
    for c0 in range(0, e, LAYER_COLS):
        cs = slice(c0, c0 + LAYER_COLS)
        part = lambda k: jnp.dot(hn, w_ref[:, k * e + c0:k * e + c0 + LAYER_COLS], preferred_element_type=F32)
        xc = part(1) * part(2)
        p1 = carry_ref[edge - 1:edge, cs]
        p2 = carry_ref[edge - 2:edge - 1, cs]
        x1 = jnp.where(rid == 0, p1, pltpu.roll(xc, 1, 0))
        x2 = jnp.where(rid == 0, p2, jnp.where(rid == 1, p1, pltpu.roll(xc, 2, 0)))
        y = cw_ref[0:1, cs] * x2 + cw_ref[1:2, cs] * x1 + cw_ref[2:3, cs] * xc
        y_ref[:, cs] = (part(0) * y * _silu(part(3))).astype(BF16)
        carry_ref[:, cs] = xc[tm - edge:, :]
        st_ref[:, cs] = xc[tm - keep:, :]

    q = jnp.dot(hn, w_ref[:, 4 * e:4 * e + XA_DIM], preferred_element_type=F32)
    _attend_into(q, k_ref, v_ref, y_ref, e)
    _residual_out(x, y_ref, wo_ref, fg_ref, o_ref, final_norm)


def conv_layer(x, ng, w, cw, init, kb, vb, layer, wo, fg, *, b, e, final_norm, tm=512):
    m, d = x.shape
    lt = m // b // tm
    mem = kb.shape[2]
    kv = pl.BlockSpec((None, None, mem, XA_DIM), lambda bi, i: (layer, bi, 0, 0))
    st = pl.BlockSpec((None, CONV_W - 1, e), lambda bi, i: (bi, 0, 0))
    return pl.pallas_call(
        functools.partial(_conv_layer_kernel, e=e, final_norm=final_norm),
        grid=(b, lt),
        in_specs=[pl.BlockSpec((tm, d), lambda bi, i: (bi * lt + i, 0)),
                  _resident((1, d)), _resident(w.shape), _resident((CONV_W, e)), st,
                  kv, kv, _resident(wo.shape), _resident((1, d))],
        out_specs=[pl.BlockSpec((tm, d), lambda bi, i: (bi * lt + i, 0)), st],
        out_shape=[jax.ShapeDtypeStruct((m, d), F32), jax.ShapeDtypeStruct((b, CONV_W - 1, e), F32)],
        scratch_shapes=[pltpu.VMEM((tm, d), BF16), pltpu.VMEM((tm, e + XA_DIM), BF16),
                        pltpu.VMEM((V7X_SUBLANES, e), F32)],
        compiler_params=_params("parallel", "arbitrary"),
        name="conv_layer",
    )(x, ng.reshape(1, d), w, cw, init, kb, vb, wo, fg.reshape(1, d))


def _s5_disc_kernel(lre_ref, lim_ref, ldt_ref, bre_ref, bim_ref, ar_ref, ai_ref, bbr_ref, bbi_ref):
    dt = jnp.exp(ldt_ref[...])
    lr, li = lre_ref[...], lim_ref[...]
    mag = jnp.exp(lr * dt)
    ar = mag * jnp.cos(li * dt)
    ai = mag * jnp.sin(li * dt)
    nr, ni = ar - 1, ai
    den = lr * lr + li * li
    cr = (nr * lr + ni * li) / den
    ci = (ni * lr - nr * li) / den
    ar_ref[...] = ar
    ai_ref[...] = ai
    cr3, ci3 = cr[:, None, :], ci[:, None, :]
    bbr_ref[...] = cr3 * bre_ref[...] - ci3 * bim_ref[...]
    bbi_ref[...] = cr3 * bim_ref[...] + ci3 * bre_ref[...]


def s5_discretize(lam_re, lam_im, log_dt, b_re, b_im):
    g, p = lam_re.shape
    gp = jax.ShapeDtypeStruct((g, p), F32)
    gip = jax.ShapeDtypeStruct(b_re.shape, F32)
    return pl.pallas_call(
        _s5_disc_kernel,
        out_shape=[gp, gp, gip, gip],
        compiler_params=pltpu.CompilerParams(vmem_limit_bytes=V7X_VMEM_LIMIT_BYTES),
        name="s5_discretize",
    )(lam_re, lam_im, log_dt.reshape(g, 1), b_re, b_im)


def _s5_kernel(u_ref, bre_ref, bim_ref, cre_ref, cim_ref, ar_ref, ai_ref, d_ref, h0r_ref, h0i_ref,
               g_ref, hr_ref, hi_ref, sr_ref, si_ref):
    tl, nb, e = u_ref.shape
    rows = tl * nb
    n_blk = bre_ref.shape[0]
    fk = bre_ref.shape[1]
    sk = bre_ref.shape[2]
    lanes = sr_ref.shape[1]

    @pl.when(pl.program_id(1) == 0)
    def _():
        hr_ref[...] = h0r_ref[...]
        hi_ref[...] = h0i_ref[...]

    u = u_ref[...].reshape(rows, e)
    ub = u.astype(BF16)
    for kb in range(n_blk):
        uk = ub[:, kb * fk:(kb + 1) * fk]
        sr_ref[:, kb * sk:(kb + 1) * sk] = jnp.dot(uk, bre_ref[kb], preferred_element_type=F32)
        si_ref[:, kb * sk:(kb + 1) * sk] = jnp.dot(uk, bim_ref[kb], preferred_element_type=F32)

    for c in range(lanes // S5_SCAN_LANES):
        ls = slice(c * S5_SCAN_LANES, (c + 1) * S5_SCAN_LANES)
        a_r = jnp.broadcast_to(ar_ref[:, ls], (V7X_SUBLANES, S5_SCAN_LANES))
        a_i = jnp.broadcast_to(ai_ref[:, ls], (V7X_SUBLANES, S5_SCAN_LANES))

        def sweep(sg, _):
            r0 = pl.multiple_of(sg * V7X_SUBLANES, V7X_SUBLANES)

            def step(t, h):
                hr, hi = h
                row = pl.multiple_of(t * nb + r0, V7X_SUBLANES)
                nr = a_r * hr - a_i * hi + sr_ref[pl.ds(row, V7X_SUBLANES), ls]
                ni = a_r * hi + a_i * hr + si_ref[pl.ds(row, V7X_SUBLANES), ls]
                sr_ref[pl.ds(row, V7X_SUBLANES), ls] = nr
                si_ref[pl.ds(row, V7X_SUBLANES), ls] = ni
                return nr, ni

            h0 = (hr_ref[pl.ds(r0, V7X_SUBLANES), ls], hi_ref[pl.ds(r0, V7X_SUBLANES), ls])
            hr, hi = lax.fori_loop(0, tl, step, h0, unroll=min(tl, 8))
            hr_ref[pl.ds(r0, V7X_SUBLANES), ls] = hr
            hi_ref[pl.ds(r0, V7X_SUBLANES), ls] = hi
            return 0

        lax.fori_loop(0, nb // V7X_SUBLANES, sweep, 0)

    for kb in range(n_blk):
        hrb = sr_ref[:, kb * sk:(kb + 1) * sk].astype(BF16)
        hib = si_ref[:, kb * sk:(kb + 1) * sk].astype(BF16)
        y = jnp.dot(hrb, cre_ref[kb], preferred_element_type=F32)
        y = y - jnp.dot(hib, cim_ref[kb], preferred_element_type=F32)
        fs = slice(kb * fk, (kb + 1) * fk)
        y = y + d_ref[:, fs] * u[:, fs]
        g_ref[:, :, fs] = jax.nn.gelu(y).reshape(tl, nb, fk)


def s5_scan(proj, bre, bim, cre, cim, ar, ai, d, h0r, h0i, *, e, tl, nbb):
    l, b, _ = proj.shape
    lanes = h0r.shape[1]
    rows = tl * nbb
    st = pl.BlockSpec((nbb, lanes), lambda bi, i: (bi, 0))
    return pl.pallas_call(
        _s5_kernel,
        grid=(b // nbb, l // tl),
        in_specs=[pl.BlockSpec((tl, nbb, e), lambda bi, i: (i, bi, 0)),
                  _resident(bre.shape), _resident(bim.shape), _resident(cre.shape), _resident(cim.shape),
                  _resident((1, lanes)), _resident((1, lanes)), _resident((1, e)), st, st],
        out_specs=[pl.BlockSpec((tl, nbb, e), lambda bi, i: (i, bi, 0)), st, st],
        out_shape=[jax.ShapeDtypeStruct((l, b, e), F32),
                   jax.ShapeDtypeStruct((b, lanes), F32), jax.ShapeDtypeStruct((b, lanes), F32)],
        scratch_shapes=[pltpu.VMEM((rows, lanes), F32), pltpu.VMEM((rows, lanes), F32)],
        compiler_params=_params("parallel", "arbitrary"),
        name="s5_scan",
    )(proj, bre, bim, cre, cim, ar.reshape(1, lanes), ai.reshape(1, lanes), d.reshape(1, e), h0r, h0i)


GLU_COLS = 512


def _glu_kernel(g_ref, w_ref, b_ref, z_ref, o_ref, gb_ref):
    gb_ref[...] = g_ref[...].astype(BF16)
    for c0 in range(0, o_ref.shape[1], GLU_COLS):
        cs = slice(c0, c0 + GLU_COLS)
        a = jnp.dot(gb_ref[...], w_ref[:, cs], preferred_element_type=F32) + b_ref[:, cs]
        o_ref[:, cs] = (g_ref[:, cs] * jax.nn.sigmoid(a) * _silu(z_ref[:, cs])).astype(BF16)


def glu_gate(g, w, bias, proj, *, e, tm=512):
    m = g.shape[0]
    tm = min(tm, m)
    return pl.pallas_call(
        _glu_kernel,
        grid=(m // tm,),
        in_specs=[pl.BlockSpec((tm, e), lambda i: (i, 0)),
                  _resident((e, e)),
                  _resident((1, e)),
                  pl.BlockSpec((tm, e), lambda i: (i, 1))],
        out_specs=pl.BlockSpec((tm, e), lambda i: (i, 0)),
        out_shape=jax.ShapeDtypeStruct((m, e), BF16),
        scratch_shapes=[pltpu.VMEM((tm, e), BF16)],
        compiler_params=_params("parallel"),
        name="glu_gate",
    )(g, w, bias.reshape(1, e), proj)


def _block_diag(x, n_blk):
    g, r, c = x.shape
    gb = g // n_blk
    eye = jnp.eye(gb, dtype=x.dtype)
    out = jnp.einsum("bgrc,gh->bgrhc", x.reshape(n_blk, gb, r, c), eye)
    return out.reshape(n_blk, gb * r, gb * c)


def kernel(x_prompt, x_sample, mem_prompt, cache_mem_k, cache_mem_v, state_conv, state_s5_re, state_s5_im,
           norm_g, final_g, mem_norm_g, w_kv, w_out, w_in_a, sgu_norm_g, sgu_w, sgu_b, w_in_b, conv_w, w_in_c,
           s5_lam_re, s5_lam_im, s5_log_dt, s5_b_re, s5_b_im, s5_c_re, s5_c_im, s5_d, w_glu, b_glu):
    bp, lp, d = x_prompt.shape
    bs, ls, _ = x_sample.shape
    depth = norm_g.shape[0]
    e = conv_w.shape[-1]
    mem_len = mem_prompt.shape[1]
    n_grp, n_state = s5_lam_re.shape[1:]
    gd = e // SGU_GROUPS
    n_blk = e // S5_FEATS_PER_BLOCK
    kinds = [i % N_MIXERS for i in range(depth)]
    kidx = [sum(1 for j in range(i) if kinds[j] == kinds[i]) for i in range(depth)]

    mk, mv, mkb, mvb = mem_kv(mem_prompt.reshape(bp * mem_len, d), mem_norm_g, w_kv.astype(BF16))
    mem_k_prompt = mk.reshape(depth, bp, mem_len, XA_HEADS, XA_HEAD_DIM)
    mem_v_prompt = mv.reshape(depth, bp, mem_len, XA_HEADS, XA_HEAD_DIM)
    mkb = mkb.reshape(depth, bp, mem_len, XA_DIM)
    mvb = mvb.reshape(depth, bp, mem_len, XA_DIM)
    ck = cache_mem_k.reshape(depth, bs, mem_len * XA_HEADS, XA_HEAD_DIM)
    cv = cache_mem_v.reshape(depth, bs, mem_len * XA_HEADS, XA_HEAD_DIM)
    q_head = jnp.arange(XA_HEADS * ls, dtype=jnp.int32)[:, None] // ls
    kv_head = jnp.arange(mem_len * XA_HEADS, dtype=jnp.int32)[None, :] % XA_HEADS
    head_bias = jnp.where(q_head == kv_head, 0.0, -1e30).astype(F32)

    xp = x_prompt.reshape(bp * lp, d)
    xs = x_sample.transpose(1, 0, 2).reshape(ls * bs, d)

    def sample_attention(proj, qcol, i):
        q = proj.reshape(ls, bs, -1)[:, :, qcol * XA_DIM:(qcol + 1) * XA_DIM]
        q = q.reshape(ls, bs, XA_HEADS, XA_HEAD_DIM).transpose(1, 2, 0, 3).reshape(bs, XA_HEADS * ls, XA_HEAD_DIM)
        xa = cross_attention_sample(q, ck, cv, head_bias, i)
        xa = xa.reshape(bs, XA_HEADS, ls, XA_HEAD_DIM).transpose(2, 0, 1, 3)
        return xa.reshape(ls * bs, XA_DIM).astype(BF16)

    conv_p, conv_s, s5r_p, s5i_p, s5r_s, s5i_s, chunk_v = [], [], [], [], [], [], []
    for i in range(depth):
        kind, j = kinds[i], kidx[i]
        wo = w_out[i].astype(BF16)
        w1, w2 = wo[:e], wo[e:]
        last = i == depth - 1
        if kind == 0:
            w = w_in_a[j].astype(BF16)
            qcol = 3 * e // XA_DIM
            bias = jnp.repeat(sgu_b[j].T, gd, axis=1)
            xp = sgu_layer(xp, norm_g[i], w, sgu_norm_g[j], sgu_w[j], bias, mkb, mvb, i, wo, final_g,
                           b=bp, e=e, final_norm=last)
            ps = norm_proj(xs, norm_g[i], w, tm=512)
            coef = jnp.repeat(sgu_w[j][:, :ls, :ls].transpose(1, 2, 0).reshape(ls * ls, SGU_GROUPS), gd, axis=1)
            ys, vn = sgu_sample(ps.reshape(ls, bs, -1), sgu_norm_g[j], coef, bias[:V7X_SUBLANES], e=e)
            chunk_v.append(vn.transpose(1, 0, 2))
            ys = ys.reshape(ls * bs, e)
            as_ = sample_attention(ps, qcol, i)
        elif kind == 1:
            w = w_in_b[j].astype(BF16)
            qcol = 4 * e // XA_DIM
            xp, cst = conv_layer(xp, norm_g[i], w, conv_w[j], jnp.zeros((bp, CONV_W - 1, e), F32), mkb, mvb, i,
                                 wo, final_g, b=bp, e=e, final_norm=last)
            conv_p.append(cst)
            ps = norm_proj(xs, norm_g[i], w, tm=512)
            ys, nst = conv_sample(ps.reshape(ls, bs, -1), conv_w[j], state_conv[j].transpose(1, 0, 2), e=e)
            conv_s.append(nst.transpose(1, 0, 2))
            ys = ys.reshape(ls * bs, e)
            as_ = sample_attention(ps, qcol, i)
        else:
            wc = w_in_c[j].astype(BF16)
            w_uz, w_q = wc[:, :2 * e], wc[:, 2 * e:]
            ar, ai, bbr, bbi = s5_discretize(s5_lam_re[j], s5_lam_im[j], s5_log_dt[j],
                                             s5_b_re[j].transpose(0, 2, 1), s5_b_im[j].transpose(0, 2, 1))
            bre = _block_diag(bbr, n_blk).astype(BF16)
            bim = _block_diag(bbi, n_blk).astype(BF16)
            cre = _block_diag(s5_c_re[j].transpose(0, 2, 1), n_blk).astype(BF16)
            cim = _block_diag(s5_c_im[j].transpose(0, 2, 1), n_blk).astype(BF16)
            ar, ai = ar.reshape(-1), ai.reshape(-1)

            xt = xp.reshape(bp, lp, d).transpose(1, 0, 2).reshape(lp * bp, d)
            pt = norm_proj(xt, norm_g[i], w_uz, tm=2048)
            pq = norm_proj(xp, norm_g[i], w_q, tm=2048)
            zero = jnp.zeros((bp, n_grp * n_state), F32)
            gp, hr, hi = s5_scan(pt.reshape(lp, bp, -1), bre, bim, cre, cim, ar, ai, s5_d[j], zero, zero,
                                 e=e, tl=32, nbb=bp)
            s5r_p.append(hr.reshape(bp, n_grp, n_state))
            s5i_p.append(hi.reshape(bp, n_grp, n_state))
            yt = glu_gate(gp.reshape(lp * bp, e), w_glu[j].astype(BF16), b_glu[j], pt, e=e)
            ap = cross_attention(pq.reshape(bp, lp, -1), 0, mkb[i], mvb[i], nb=1, tq=512, out_dtype=BF16)
            at = ap.transpose(1, 0, 2).reshape(lp * bp, XA_DIM)
            xt = out_proj(yt, at, w1, w2, xt, final_g, final_norm=last)
            xp = xt.reshape(lp, bp, d).transpose(1, 0, 2).reshape(bp * lp, d)

            ps = norm_proj(xs, norm_g[i], wc, tm=512)
            gs, hr, hi = s5_scan(ps.reshape(ls, bs, -1), bre, bim, cre, cim, ar, ai, s5_d[j],
                                 state_s5_re[j].reshape(bs, -1), state_s5_im[j].reshape(bs, -1),
                                 e=e, tl=ls, nbb=32)
            s5r_s.append(hr.reshape(bs, n_grp, n_state))
            s5i_s.append(hi.reshape(bs, n_grp, n_state))
            ys = glu_gate(gs.reshape(ls * bs, e), w_glu[j].astype(BF16), b_glu[j], ps, e=e)
            as_ = sample_attention(ps, 2 * e // XA_DIM, i)
        xs = out_proj(ys, as_, w1, w2, xs, final_g, final_norm=last)

    y_prompt = xp.reshape(bp, lp, d)
    y_sample = xs.reshape(ls, bs, d).transpose(1, 0, 2)
    return (y_prompt, y_sample, mem_k_prompt, mem_v_prompt, jnp.stack(conv_p), jnp.stack(conv_s),
            jnp.stack(s5r_p), jnp.stack(s5i_p), jnp.stack(s5r_s), jnp.stack(s5i_s), jnp.stack(chunk_v))
```

```python
import functools

import jax
import jax.numpy as jnp
from jax import lax
from jax.experimental import pallas as pl
from jax.experimental.pallas import tpu as pltpu

F32 = jnp.float32
BF16 = jnp.bfloat16

EPS = 1e-6
N_MIXERS = 3
CHUNK = 128
SGU_GROUPS = 8
CONV_W = 3
S5_IN = 16
XA_HEADS = 4
XA_HEAD_DIM = 128
XA_DIM = XA_HEADS * XA_HEAD_DIM

V7X_SUBLANES = 8
V7X_VMEM_LIMIT_BYTES = 56 * 1024 * 1024
S5_FEATS_PER_BLOCK = 256
S5_SCAN_LANES = 512


def _params(*semantics):
    return pltpu.CompilerParams(dimension_semantics=semantics, vmem_limit_bytes=V7X_VMEM_LIMIT_BYTES)


def _resident(shape):
    nd = len(shape)
    return pl.BlockSpec(shape, lambda *_: (0,) * nd, pipeline_mode=pl.Buffered(1))


def _rms(x, g):
    return x * lax.rsqrt(jnp.mean(x * x, axis=-1, keepdims=True) + EPS) * g


def _silu(z):
    return z * jax.nn.sigmoid(z)


def _norm_proj_kernel(x_ref, g_ref, w_ref, o_ref, hn_ref):
    @pl.when(pl.program_id(1) == 0)
    def _():
        hn_ref[...] = _rms(x_ref[...], g_ref[...]).astype(BF16)

    o_ref[...] = jnp.dot(hn_ref[...], w_ref[...], preferred_element_type=F32)


def norm_proj(x, g, w, *, tm, tn=512):
    m, d = x.shape
    n = w.shape[1]
    tm = min(tm, m)
    return pl.pallas_call(
        _norm_proj_kernel,
        grid=(m // tm, n // tn),
        in_specs=[pl.BlockSpec((tm, d), lambda i, j: (i, 0)),
                  pl.BlockSpec((1, d), lambda i, j: (0, 0)),
                  pl.BlockSpec((d, tn), lambda i, j: (0, j))],
        out_specs=pl.BlockSpec((tm, tn), lambda i, j: (i, j)),
        out_shape=jax.ShapeDtypeStruct((m, n), F32),
        scratch_shapes=[pltpu.VMEM((tm, d), BF16)],
        compiler_params=_params("parallel", "arbitrary"),
        name="norm_proj",
    )(x, g.reshape(1, d), w)


def _mem_kv_kernel(x_ref, g_ref, w_ref, k_ref, v_ref, kb_ref, vb_ref):
    hn = _rms(x_ref[...], g_ref[0]).astype(BF16)
    kv = jnp.dot(hn, w_ref[0], preferred_element_type=F32)
    k, v = kv[:, :XA_DIM], kv[:, XA_DIM:]
    k_ref[0] = k
    v_ref[0] = v
    kb_ref[0] = k.astype(BF16)
    vb_ref[0] = v.astype(BF16)


def mem_kv(mem, g, w, *, tm=512):
    m, d = mem.shape
    depth = g.shape[0]
    out = pl.BlockSpec((1, tm, XA_DIM), lambda l, i: (l, i, 0))
    shape = lambda dt: jax.ShapeDtypeStruct((depth, m, XA_DIM), dt)
    return pl.pallas_call(
        _mem_kv_kernel,
        grid=(depth, m // tm),
        in_specs=[pl.BlockSpec((tm, d), lambda l, i: (i, 0)),
                  pl.BlockSpec((1, 1, d), lambda l, i: (l, 0, 0)),
                  pl.BlockSpec((1, d, 2 * XA_DIM), lambda l, i: (l, 0, 0))],
        out_specs=[out, out, out, out],
        out_shape=[shape(F32), shape(F32), shape(BF16), shape(BF16)],
        compiler_params=_params("parallel", "parallel"),
        name="mem_kv",
    )(mem, g.reshape(depth, 1, d), w)


def _attn_kernel(q_ref, k_ref, v_ref, o_ref, *, nb):
    scale = XA_HEAD_DIM ** -0.5
    for b in range(nb):
        q = q_ref[b].astype(BF16)
        k = k_ref[b].astype(BF16)
        v = v_ref[b].astype(BF16)
        for h in range(XA_HEADS):
            hs = slice(h * XA_HEAD_DIM, (h + 1) * XA_HEAD_DIM)
            s = lax.dot_general(q[:, hs], k[:, hs], (((1,), (1,)), ((), ())),
                                preferred_element_type=F32) * scale
            e = jnp.exp(s - jnp.max(s, axis=-1, keepdims=True))
            p = e / jnp.sum(e, axis=-1, keepdims=True)
            o = jnp.dot(p.astype(BF16), v[:, hs], preferred_element_type=F32)
            o_ref[b, :, hs] = o.astype(o_ref.dtype)


def cross_attention(q, qcol, k, v, *, nb, tq, out_dtype):
    b, l, _ = q.shape
    mem = k.shape[1]
    return pl.pallas_call(
        functools.partial(_attn_kernel, nb=nb),
        grid=(b // nb, l // tq),
        in_specs=[pl.BlockSpec((nb, tq, XA_DIM), lambda i, j: (i, j, qcol)),
                  pl.BlockSpec((nb, mem, XA_DIM), lambda i, j: (i, 0, 0)),
                  pl.BlockSpec((nb, mem, XA_DIM), lambda i, j: (i, 0, 0))],
        out_specs=pl.BlockSpec((nb, tq, XA_DIM), lambda i, j: (i, j, 0)),
        out_shape=jax.ShapeDtypeStruct((b, l, XA_DIM), out_dtype),
        compiler_params=_params("parallel", "arbitrary"),
        name="cross_attention",
    )(q, k, v)


def _attn_sample_kernel(q_ref, k_ref, v_ref, bias_ref, o_ref, *, nb):
    scale = XA_HEAD_DIM ** -0.5
    for b in range(nb):
        q = q_ref[b].astype(BF16)
        k = k_ref[b].astype(BF16)
        v = v_ref[b].astype(BF16)
        s = lax.dot_general(q, k, (((1,), (1,)), ((), ())), preferred_element_type=F32) * scale + bias_ref[...]
        e = jnp.exp(s - jnp.max(s, axis=-1, keepdims=True))
        p = e / jnp.sum(e, axis=-1, keepdims=True)
        o_ref[b] = jnp.dot(p.astype(BF16), v, preferred_element_type=F32)


def cross_attention_sample(q, cache_k, cache_v, bias, layer, *, nb=8):
    b, nq, dh = q.shape
    nkv = cache_k.shape[2]
    kv = pl.BlockSpec((None, nb, nkv, dh), lambda i: (layer, i, 0, 0))
    return pl.pallas_call(
        functools.partial(_attn_sample_kernel, nb=nb),
        grid=(b // nb,),
        in_specs=[pl.BlockSpec((nb, nq, dh), lambda i: (i, 0, 0)), kv, kv, _resident((nq, nkv))],
        out_specs=pl.BlockSpec((nb, nq, dh), lambda i: (i, 0, 0)),
        out_shape=jax.ShapeDtypeStruct((b, nq, dh), F32),
        compiler_params=_params("parallel"),
        name="cross_attention_sample",
    )(q, cache_k, cache_v, bias)


def _out_proj_kernel(y_ref, xa_ref, w1_ref, w2_ref, x_ref, fg_ref, o_ref, *, final_norm):
    acc = x_ref[...] + jnp.dot(y_ref[...], w1_ref[...], preferred_element_type=F32)
    acc = acc + jnp.dot(xa_ref[...], w2_ref[...], preferred_element_type=F32)
    o_ref[...] = _rms(acc, fg_ref[...]) if final_norm else acc


def out_proj(y, xa, w1, w2, x, final_g, *, final_norm, tm=512):
    m, d = x.shape
    e, a = y.shape[1], xa.shape[1]
    tm = min(tm, m)
    return pl.pallas_call(
        functools.partial(_out_proj_kernel, final_norm=final_norm),
        grid=(m // tm,),
        in_specs=[pl.BlockSpec((tm, e), lambda i: (i, 0)),
                  pl.BlockSpec((tm, a), lambda i: (i, 0)),
                  _resident((e, d)),
                  _resident((a, d)),
                  pl.BlockSpec((tm, d), lambda i: (i, 0)),
                  _resident((1, d))],
        out_specs=pl.BlockSpec((tm, d), lambda i: (i, 0)),
        out_shape=jax.ShapeDtypeStruct((m, d), F32),
        compiler_params=_params("parallel"),
        name="out_proj",
    )(y, xa, w1, w2, x, final_g.reshape(1, d))


def _sgu_sample_kernel(u_ref, v_ref, z_ref, g_ref, coef_ref, bias_ref, o_ref, vn_ref):
    steps = u_ref.shape[0]
    vn = []
    for t in range(steps):
        n = _rms(v_ref[t], g_ref[...])
        vn_ref[t] = n
        vn.append(n)
    for t in range(steps):
        mixed = bias_ref[t:t + 1, :]
        for s in range(t + 1):
            mixed = mixed + coef_ref[t * steps + s:t * steps + s + 1, :] * vn[s]
        o_ref[t] = (u_ref[t] * mixed * _silu(z_ref[t])).astype(BF16)


def sgu_sample(proj, g, coef, bias, *, e, nbb=32):
    t, b, _ = proj.shape
    col = lambda c: pl.BlockSpec((t, nbb, e), lambda i: (0, i, c))
    out = pl.BlockSpec((t, nbb, e), lambda i: (0, i, 0))
    return pl.pallas_call(
        _sgu_sample_kernel,
        grid=(b // nbb,),
        in_specs=[col(0), col(1), col(2), _resident((1, e)), _resident(coef.shape), _resident(bias.shape)],
        out_specs=[out, out],
        out_shape=[jax.ShapeDtypeStruct((t, b, e), BF16), jax.ShapeDtypeStruct((t, b, e), F32)],
        compiler_params=_params("parallel"),
        name="sgu_sample",
    )(proj, proj, proj, g.reshape(1, e), coef, bias)


def _conv_sample_kernel(bg_ref, cg_ref, hv_ref, z_ref, w_ref, st_ref, o_ref, ns_ref):
    steps = cg_ref.shape[0]
    xp = [st_ref[k] for k in range(CONV_W - 1)] + [cg_ref[t] * hv_ref[t] for t in range(steps)]
    for t in range(steps):
        y = w_ref[0:1, :] * xp[t]
        for k in range(1, CONV_W):
            y = y + w_ref[k:k + 1, :] * xp[t + k]
        o_ref[t] = (bg_ref[t] * y * _silu(z_ref[t])).astype(BF16)
    for k in range(CONV_W - 1):
        ns_ref[k] = xp[steps + k]


def conv_sample(proj, w, state, *, e, nbb=32):
    t, b, _ = proj.shape
    col = lambda c: pl.BlockSpec((t, nbb, e), lambda i: (0, i, c))
    st = pl.BlockSpec((CONV_W - 1, nbb, e), lambda i: (0, i, 0))
    return pl.pallas_call(
        _conv_sample_kernel,
        grid=(b // nbb,),
        in_specs=[col(0), col(1), col(2), col(3), _resident((CONV_W, e)), st],
        out_specs=[pl.BlockSpec((t, nbb, e), lambda i: (0, i, 0)), st],
        out_shape=[jax.ShapeDtypeStruct((t, b, e), BF16), jax.ShapeDtypeStruct((CONV_W - 1, b, e), F32)],
        compiler_params=_params("parallel"),
        name="conv_sample",
    )(proj, proj, proj, proj, w, state)


LAYER_COLS = 256


def _attend_into(q, k_ref, v_ref, y_ref, col0):
    scale = XA_HEAD_DIM ** -0.5
    qb = q.astype(BF16)
    for h in range(XA_HEADS):
        hs = slice(h * XA_HEAD_DIM, (h + 1) * XA_HEAD_DIM)
        s = lax.dot_general(qb[:, hs], k_ref[:, hs], (((1,), (1,)), ((), ())),
                            preferred_element_type=F32) * scale
        ex = jnp.exp(s - jnp.max(s, axis=-1, keepdims=True))
        p = ex / jnp.sum(ex, axis=-1, keepdims=True)
        o = jnp.dot(p.astype(BF16), v_ref[:, hs], preferred_element_type=F32)
        y_ref[:, col0 + h * XA_HEAD_DIM:col0 + (h + 1) * XA_HEAD_DIM] = o.astype(BF16)


def _residual_out(x, y_ref, wo_ref, fg_ref, o_ref, final_norm):
    acc = x + jnp.dot(y_ref[...], wo_ref[...], preferred_element_type=F32)
    o_ref[...] = _rms(acc, fg_ref[...]) if final_norm else acc


def _sgu_layer_kernel(x_ref, ng_ref, w_ref, sg_ref, ws_ref, bias_ref, k_ref, v_ref, wo_ref, fg_ref, o_ref,
                      hn_ref, vn_ref, y_ref, *, e, final_norm):
    tm = x_ref.shape[0]
    gd = e // SGU_GROUPS
    x = x_ref[...]
    hn_ref[...] = _rms(x, ng_ref[...]).astype(BF16)
    hn = hn_ref[...]

    for c0 in range(0, e, LAYER_COLS):
        vn_ref[:, c0:c0 + LAYER_COLS] = jnp.dot(hn, w_ref[:, e + c0:e + c0 + LAYER_COLS],
                                                preferred_element_type=F32)
    vn_ref[...] = _rms(vn_ref[...], sg_ref[...])

    row = lax.broadcasted_iota(jnp.int32, (CHUNK, CHUNK), 0)
    col = lax.broadcasted_iota(jnp.int32, (CHUNK, CHUNK), 1)
    for g in range(SGU_GROUPS):
        cs = slice(g * gd, (g + 1) * gd)
        wg = jnp.where(row >= col, ws_ref[g], 0.0).astype(BF16)
        u = jnp.dot(hn, w_ref[:, g * gd:(g + 1) * gd], preferred_element_type=F32)
        z = jnp.dot(hn, w_ref[:, 2 * e + g * gd:2 * e + (g + 1) * gd], preferred_element_type=F32)
        for c in range(tm // CHUNK):
            rs = slice(c * CHUNK, (c + 1) * CHUNK)
            mixed = jnp.dot(wg, vn_ref[rs, cs].astype(BF16), preferred_element_type=F32) + bias_ref[:, cs]
            y_ref[rs, cs] = (u[rs] * mixed * _silu(z[rs])).astype(BF16)

    q = jnp.dot(hn, w_ref[:, 3 * e:3 * e + XA_DIM], preferred_element_type=F32)
    _attend_into(q, k_ref, v_ref, y_ref, e)
    _residual_out(x, y_ref, wo_ref, fg_ref, o_ref, final_norm)


def sgu_layer(x, ng, w, sg, ws, bias, kb, vb, layer, wo, fg, *, b, e, final_norm, tm=512):
    m, d = x.shape
    lt = m // b // tm
    mem = kb.shape[2]
    kv = pl.BlockSpec((None, None, mem, XA_DIM), lambda bi, i: (layer, bi, 0, 0))
    return pl.pallas_call(
        functools.partial(_sgu_layer_kernel, e=e, final_norm=final_norm),
        grid=(b, lt),
        in_specs=[pl.BlockSpec((tm, d), lambda bi, i: (bi * lt + i, 0)),
                  _resident((1, d)), _resident(w.shape), _resident((1, e)), _resident(ws.shape),
                  _resident(bias.shape), kv, kv, _resident(wo.shape), _resident((1, d))],
        out_specs=pl.BlockSpec((tm, d), lambda bi, i: (bi * lt + i, 0)),
        out_shape=jax.ShapeDtypeStruct((m, d), F32),
        scratch_shapes=[pltpu.VMEM((tm, d), BF16), pltpu.VMEM((tm, e), F32),
                        pltpu.VMEM((tm, e + XA_DIM), BF16)],
        compiler_params=_params("parallel", "arbitrary"),
        name="sgu_layer",
    )(x, ng.reshape(1, d), w, sg.reshape(1, e), ws, bias, kb, vb, wo, fg.reshape(1, d))


def _conv_layer_kernel(x_ref, ng_ref, w_ref, cw_ref, init_ref, k_ref, v_ref, wo_ref, fg_ref, o_ref, st_ref,
                       hn_ref, y_ref, carry_ref, *, e, final_norm):
    tm = x_ref.shape[0]
    keep = CONV_W - 1
    edge = carry_ref.shape[0]

    @pl.when(pl.program_id(1) == 0)
    def _():
        carry_ref[...] = jnp.zeros_like(carry_ref)
        carry_ref[edge - keep:edge, :] = init_ref[...]

    x = x_ref[...]
    hn_ref[...] = _rms(x, ng_ref[...]).astype(BF16)
    hn = hn_ref[...]
    rid = lax.broadcasted_iota(jnp.int32, (tm, LAYER_COLS), 0)
    for c0 in range(0, e, LAYER_COLS):
        cs = slice(c0, c0 + LAYER_COLS)
        part = lambda k: jnp.dot(hn, w_ref[:, k * e + c0:k * e + c0 + LAYER_COLS], preferred_element_type=F32)
        xc = part(1) * part(2)
        p1 = carry_ref[edge - 1:edge, cs]
        p2 = carry_ref[edge - 2:edge - 1, cs]
        x1 = jnp.where(rid == 0, p1, pltpu.roll(xc, 1, 0))
        x2 = jnp.where(rid == 0, p2, jnp.where(rid == 1, p1, pltpu.roll(xc, 2, 0)))
        y = cw_ref[0:1, cs] * x2 + cw_ref[1:2, cs] * x1 + cw_ref[2:3, cs] * xc
        y_ref[:, cs] = (part(0) * y * _silu(part(3))).astype(BF16)
        carry_ref[:, cs] = xc[tm - edge:, :]
        st_ref[:, cs] = xc[tm - keep:, :]

    q = jnp.dot(hn, w_ref[:, 4 * e:4 * e + XA_DIM], preferred_element_type=F32)
    _attend_into(q, k_ref, v_ref, y_ref, e)
    _residual_out(x, y_ref, wo_ref, fg_ref, o_ref, final_norm)


def conv_layer(x, ng, w, cw, init, kb, vb, layer, wo, fg, *, b, e, final_norm, tm=512):
    m, d = x.shape
    lt = m // b // tm
    mem = kb.shape[2]
    kv = pl.BlockSpec((None, None, mem, XA_DIM), lambda bi, i: (layer, bi, 0, 0))
    st = pl.BlockSpec((None, CONV_W - 1, e), lambda bi, i: (bi, 0, 0))
    return pl.pallas_call(
        functools.partial(_conv_layer_kernel, e=e, final_norm=final_norm),
        grid=(b, lt),
        in_specs=[pl.BlockSpec((tm, d), lambda bi, i: (bi * lt + i, 0)),
                  _resident((1, d)), _resident(w.shape), _resident((CONV_W, e)), st,
                  kv, kv, _resident(wo.shape), _resident((1, d))],
        out_specs=[pl.BlockSpec((tm, d), lambda bi, i: (bi * lt + i, 0)), st],
        out_shape=[jax.ShapeDtypeStruct((m, d), F32), jax.ShapeDtypeStruct((b, CONV_W - 1, e), F32)],
        scratch_shapes=[pltpu.VMEM((tm, d), BF16), pltpu.VMEM((tm, e + XA_DIM), BF16),
                        pltpu.VMEM((V7X_SUBLANES, e), F32)],
        compiler_params=_params("parallel", "arbitrary"),
        name="conv_layer",
    )(x, ng.reshape(1, d), w, cw, init, kb, vb, wo, fg.reshape(1, d))


def _s5_disc_kernel(lre_ref, lim_ref, ldt_ref, bre_ref, bim_ref, ar_ref, ai_ref, bbr_ref, bbi_ref):
    dt = jnp.exp(ldt_ref[...])
    lr, li = lre_ref[...], lim_ref[...]
    mag = jnp.exp(lr * dt)
    ar = mag * jnp.cos(li * dt)
    ai = mag * jnp.sin(li * dt)
    nr, ni = ar - 1, ai
    den = lr * lr + li * li
    cr = (nr * lr + ni * li) / den
    ci = (ni * lr - nr * li) / den
    ar_ref[...] = ar
    ai_ref[...] = ai
    cr3, ci3 = cr[:, None, :], ci[:, None, :]
    bbr_ref[...] = cr3 * bre_ref[...] - ci3 * bim_ref[...]
    bbi_ref[...] = cr3 * bim_ref[...] + ci3 * bre_ref[...]


def s5_discretize(lam_re, lam_im, log_dt, b_re, b_im):
    g, p = lam_re.shape
    gp = jax.ShapeDtypeStruct((g, p), F32)
    gip = jax.ShapeDtypeStruct(b_re.shape, F32)
    return pl.pallas_call(
        _s5_disc_kernel,
        out_shape=[gp, gp, gip, gip],
        compiler_params=pltpu.CompilerParams(vmem_limit_bytes=V7X_VMEM_LIMIT_BYTES),
        name="s5_discretize",
    )(lam_re, lam_im, log_dt.reshape(g, 1), b_re, b_im)


def _s5_kernel(u_ref, bre_ref, bim_ref, cre_ref, cim_ref, ar_ref, ai_ref, d_ref, h0r_ref, h0i_ref,
               g_ref, hr_ref, hi_ref, sr_ref, si_ref):
    tl, nb, e = u_ref.shape
    rows = tl * nb
    n_blk = bre_ref.shape[0]
    fk = bre_ref.shape[1]
    sk = bre_ref.shape[2]
    lanes = sr_ref.shape[1]

    @pl.when(pl.program_id(1) == 0)
    def _():
        hr_ref[...] = h0r_ref[...]
        hi_ref[...] = h0i_ref[...]

    u = u_ref[...].reshape(rows, e)
    ub = u.astype(BF16)
    for kb in range(n_blk):
        bs = slice(kb * sk, (kb + 1) * sk)
        fs = slice(kb * fk, (kb + 1) * fk)
        sr_ref[:, bs] = jnp.dot(ub[:, fs], bre_ref[kb], preferred_element_type=F32)
        si_ref[:, bs] = jnp.dot(ub[:, fs], bim_ref[kb], preferred_element_type=F32)

        for c0 in range(kb * sk, (kb + 1) * sk, S5_SCAN_LANES):
            ls = slice(c0, c0 + S5_SCAN_LANES)
            a_r = jnp.broadcast_to(ar_ref[:, ls], (V7X_SUBLANES, S5_SCAN_LANES))
            a_i = jnp.broadcast_to(ai_ref[:, ls], (V7X_SUBLANES, S5_SCAN_LANES))
            for r0 in range(0, nb, V7X_SUBLANES):
                hr = hr_ref[r0:r0 + V7X_SUBLANES, ls]
                hi = hi_ref[r0:r0 + V7X_SUBLANES, ls]
                for t in range(tl):
                    rs = slice(t * nb + r0, t * nb + r0 + V7X_SUBLANES)
                    hr, hi = (a_r * hr - a_i * hi + sr_ref[rs, ls],
                              a_r * hi + a_i * hr + si_ref[rs, ls])
                    sr_ref[rs, ls] = hr
                    si_ref[rs, ls] = hi
                hr_ref[r0:r0 + V7X_SUBLANES, ls] = hr
                hi_ref[r0:r0 + V7X_SUBLANES, ls] = hi

        y = jnp.dot(sr_ref[:, bs].astype(BF16), cre_ref[kb], preferred_element_type=F32)
        y = y - jnp.dot(si_ref[:, bs].astype(BF16), cim_ref[kb], preferred_element_type=F32)
        y = y + d_ref[:, fs] * u[:, fs]
        g_ref[:, :, fs] = jax.nn.gelu(y).reshape(tl, nb, fk)


def s5_scan(proj, bre, bim, cre, cim, ar, ai, d, h0r, h0i, *, e, tl, nbb):
    l, b, _ = proj.shape
    lanes = h0r.shape[1]
    rows = tl * nbb
    st = pl.BlockSpec((nbb, lanes), lambda bi, i: (bi, 0))
    return pl.pallas_call(
        _s5_kernel,
        grid=(b // nbb, l // tl),
        in_specs=[pl.BlockSpec((tl, nbb, e), lambda bi, i: (i, bi, 0)),
                  _resident(bre.shape), _resident(bim.shape), _resident(cre.shape), _resident(cim.shape),
                  _resident((1, lanes)), _resident((1, lanes)), _resident((1, e)), st, st],
        out_specs=[pl.BlockSpec((tl, nbb, e), lambda bi, i: (i, bi, 0)), st, st],
        out_shape=[jax.ShapeDtypeStruct((l, b, e), F32),
                   jax.ShapeDtypeStruct((b, lanes), F32), jax.ShapeDtypeStruct((b, lanes), F32)],
        scratch_shapes=[pltpu.VMEM((rows, lanes), F32), pltpu.VMEM((rows, lanes), F32)],
        compiler_params=_params("parallel", "arbitrary"),
        name="s5_scan",
    )(proj, bre, bim, cre, cim, ar.reshape(1, lanes), ai.reshape(1, lanes), d.reshape(1, e), h0r, h0i)


GLU_COLS = 512


def _glu_kernel(g_ref, w_ref, b_ref, z_ref, o_ref, gb_ref):
    gb_ref[...] = g_ref[...].astype(BF16)
    for c0 in range(0, o_ref.shape[1], GLU_COLS):
        cs = slice(c0, c0 + GLU_COLS)
        a = jnp.dot(gb_ref[...], w_ref[:, cs], preferred_element_type=F32) + b_ref[:, cs]
        o_ref[:, cs] = (g_ref[:, cs] * jax.nn.sigmoid(a) * _silu(z_ref[:, cs])).astype(BF16)


def glu_gate(g, w, bias, proj, *, e, tm=512):
    m = g.shape[0]
    tm = min(tm, m)
    return pl.pallas_call(
        _glu_kernel,
        grid=(m // tm,),
        in_specs=[pl.BlockSpec((tm, e), lambda i: (i, 0)),
                  _resident((e, e)),
                  _resident((1, e)),
                  pl.BlockSpec((tm, e), lambda i: (i, 1))],
        out_specs=pl.BlockSpec((tm, e), lambda i: (i, 0)),
        out_shape=jax.ShapeDtypeStruct((m, e), BF16),
        scratch_shapes=[pltpu.VMEM((tm, e), BF16)],
        compiler_params=_params("parallel"),
        name="glu_gate",
    )(g, w, bias.reshape(1, e), proj)


def _block_diag(x, n_blk):
    g, r, c = x.shape
    gb = g // n_blk
    eye = jnp.eye(gb, dtype=x.dtype)
    out = jnp.einsum("bgrc,gh->bgrhc", x.reshape(n_blk, gb, r, c), eye)
    return out.reshape(n_blk, gb * r, gb * c)


def kernel(x_prompt, x_sample, mem_prompt, cache_mem_k, cache_mem_v, state_conv, state_s5_re, state_s5_im,
           norm_g, final_g, mem_norm_g, w_kv, w_out, w_in_a, sgu_norm_g, sgu_w, sgu_b, w_in_b, conv_w, w_in_c,
           s5_lam_re, s5_lam_im, s5_log_dt, s5_b_re, s5_b_im, s5_c_re, s5_c_im, s5_d, w_glu, b_glu):
    bp, lp, d = x_prompt.shape
    bs, ls, _ = x_sample.shape
    depth = norm_g.shape[0]
    e = conv_w.shape[-1]
    mem_len = mem_prompt.shape[1]
    n_grp, n_state = s5_lam_re.shape[1:]
    gd = e // SGU_GROUPS
    n_blk = e // S5_FEATS_PER_BLOCK
    kinds = [i % N_MIXERS for i in range(depth)]
    kidx = [sum(1 for j in range(i) if kinds[j] == kinds[i]) for i in range(depth)]

    mk, mv, mkb, mvb = mem_kv(mem_prompt.reshape(bp * mem_len, d), mem_norm_g, w_kv.astype(BF16))
    mem_k_prompt = mk.reshape(depth, bp, mem_len, XA_HEADS, XA_HEAD_DIM)
    mem_v_prompt = mv.reshape(depth, bp, mem_len, XA_HEADS, XA_HEAD_DIM)
    mkb = mkb.reshape(depth, bp, mem_len, XA_DIM)
    mvb = mvb.reshape(depth, bp, mem_len, XA_DIM)
    ck = cache_mem_k.reshape(depth, bs, mem_len * XA_HEADS, XA_HEAD_DIM)
    cv = cache_mem_v.reshape(depth, bs, mem_len * XA_HEADS, XA_HEAD_DIM)
    q_head = jnp.arange(XA_HEADS * ls, dtype=jnp.int32)[:, None] // ls
    kv_head = jnp.arange(mem_len * XA_HEADS, dtype=jnp.int32)[None, :] % XA_HEADS
    head_bias = jnp.where(q_head == kv_head, 0.0, -1e30).astype(F32)

    xp = x_prompt.reshape(bp * lp, d)
    xs = x_sample.transpose(1, 0, 2).reshape(ls * bs, d)

    def sample_attention(proj, qcol, i):
        q = proj.reshape(ls, bs, -1)[:, :, qcol * XA_DIM:(qcol + 1) * XA_DIM]
        q = q.reshape(ls, bs, XA_HEADS, XA_HEAD_DIM).transpose(1, 2, 0, 3).reshape(bs, XA_HEADS * ls, XA_HEAD_DIM)
        xa = cross_attention_sample(q, ck, cv, head_bias, i)
        xa = xa.reshape(bs, XA_HEADS, ls, XA_HEAD_DIM).transpose(2, 0, 1, 3)
        return xa.reshape(ls * bs, XA_DIM).astype(BF16)

    conv_p, conv_s, s5r_p, s5i_p, s5r_s, s5i_s, chunk_v = [], [], [], [], [], [], []
    for i in range(depth):
        kind, j = kinds[i], kidx[i]
        wo = w_out[i].astype(BF16)
        w1, w2 = wo[:e], wo[e:]
        last = i == depth - 1
        if kind == 0:
            w = w_in_a[j].astype(BF16)
            qcol = 3 * e // XA_DIM
            bias = jnp.repeat(sgu_b[j].T, gd, axis=1)
            xp = sgu_layer(xp, norm_g[i], w, sgu_norm_g[j], sgu_w[j], bias, mkb, mvb, i, wo, final_g,
                           b=bp, e=e, final_norm=last)
            ps = norm_proj(xs, norm_g[i], w, tm=512)
            coef = jnp.repeat(sgu_w[j][:, :ls, :ls].transpose(1, 2, 0).reshape(ls * ls, SGU_GROUPS), gd, axis=1)
            ys, vn = sgu_sample(ps.reshape(ls, bs, -1), sgu_norm_g[j], coef, bias[:V7X_SUBLANES], e=e)
            chunk_v.append(vn.transpose(1, 0, 2))
            ys = ys.reshape(ls * bs, e)
            as_ = sample_attention(ps, qcol, i)
        elif kind == 1:
            w = w_in_b[j].astype(BF16)
            qcol = 4 * e // XA_DIM
            xp, cst = conv_layer(xp, norm_g[i], w, conv_w[j], jnp.zeros((bp, CONV_W - 1, e), F32), mkb, mvb, i,
                                 wo, final_g, b=bp, e=e, final_norm=last)
            conv_p.append(cst)
            ps = norm_proj(xs, norm_g[i], w, tm=512)
            ys, nst = conv_sample(ps.reshape(ls, bs, -1), conv_w[j], state_conv[j].transpose(1, 0, 2), e=e)
            conv_s.append(nst.transpose(1, 0, 2))
            ys = ys.reshape(ls * bs, e)
            as_ = sample_attention(ps, qcol, i)
        else:
            wc = w_in_c[j].astype(BF16)
            w_uz, w_q = wc[:, :2 * e], wc[:, 2 * e:]
            ar, ai, bbr, bbi = s5_discretize(s5_lam_re[j], s5_lam_im[j], s5_log_dt[j],
                                             s5_b_re[j].transpose(0, 2, 1), s5_b_im[j].transpose(0, 2, 1))
            bre = _block_diag(bbr, n_blk).astype(BF16)
            bim = _block_diag(bbi, n_blk).astype(BF16)
            cre = _block_diag(s5_c_re[j].transpose(0, 2, 1), n_blk).astype(BF16)
            cim = _block_diag(s5_c_im[j].transpose(0, 2, 1), n_blk).astype(BF16)
            ar, ai = ar.reshape(-1), ai.reshape(-1)

            xt = xp.reshape(bp, lp, d).transpose(1, 0, 2).reshape(lp * bp, d)
            pt = norm_proj(xt, norm_g[i], w_uz, tm=2048)
            pq = norm_proj(xp, norm_g[i], w_q, tm=2048)
            zero = jnp.zeros((bp, n_grp * n_state), F32)
            gp, hr, hi = s5_scan(pt.reshape(lp, bp, -1), bre, bim, cre, cim, ar, ai, s5_d[j], zero, zero,
                                 e=e, tl=32, nbb=bp)
            s5r_p.append(hr.reshape(bp, n_grp, n_state))
            s5i_p.append(hi.reshape(bp, n_grp, n_state))
            yt = glu_gate(gp.reshape(lp * bp, e), w_glu[j].astype(BF16), b_glu[j], pt, e=e)
            ap = cross_attention(pq.reshape(bp, lp, -1), 0, mkb[i], mvb[i], nb=1, tq=512, out_dtype=BF16)
            at = ap.transpose(1, 0, 2).reshape(lp * bp, XA_DIM)
            xt = out_proj(yt, at, w1, w2, xt, final_g, final_norm=last)
            xp = xt.reshape(lp, bp, d).transpose(1, 0, 2).reshape(bp * lp, d)

            ps = norm_proj(xs, norm_g[i], wc, tm=512)
            gs, hr, hi = s5_scan(ps.reshape(ls, bs, -1), bre, bim, cre, cim, ar, ai, s5_d[j],
                                 state_s5_re[j].reshape(bs, -1), state_s5_im[j].reshape(bs, -1),
                                 e=e, tl=ls, nbb=32)
            s5r_s.append(hr.reshape(bs, n_grp, n_state))
            s5i_s.append(hi.reshape(bs, n_grp, n_state))
            ys = glu_gate(gs.reshape(ls * bs, e), w_glu[j].astype(BF16), b_glu[j], ps, e=e)
            as_ = sample_attention(ps, 2 * e // XA_DIM, i)
        xs = out_proj(ys, as_, w1, w2, xs, final_g, final_norm=last)

    y_prompt = xp.reshape(bp, lp, d)
    y_sample = xs.reshape(ls, bs, d).transpose(1, 0, 2)
    return (y_prompt, y_sample, mem_k_prompt, mem_v_prompt, jnp.stack(conv_p), jnp.stack(conv_s),
            jnp.stack(s5r_p), jnp.stack(s5i_p), jnp.stack(s5r_s), jnp.stack(s5i_s), jnp.stack(chunk_v))
```

```python
import functools

import jax
import jax.numpy as jnp
from jax import lax
from jax.experimental import pallas as pl
from jax.experimental.pallas import tpu as pltpu

F32 = jnp.float32
BF16 = jnp.bfloat16

EPS = 1e-6
N_MIXERS = 3
CHUNK = 128
SGU_GROUPS = 8
CONV_W = 3
S5_IN = 16
XA_HEADS = 4
XA_HEAD_DIM = 128
XA_DIM = XA_HEADS * XA_HEAD_DIM

V7X_SUBLANES = 8
V7X_VMEM_LIMIT_BYTES = 56 * 1024 * 1024
S5_FEATS_PER_BLOCK = 256
S5_SCAN_LANES = 512


def _params(*semantics):
    return pltpu.CompilerParams(dimension_semantics=semantics, vmem_limit_bytes=V7X_VMEM_LIMIT_BYTES)


def _resident(shape):
    nd = len(shape)
    return pl.BlockSpec(shape, lambda *_: (0,) * nd, pipeline_mode=pl.Buffered(1))


def _rms(x, g):
    return x * lax.rsqrt(jnp.mean(x * x, axis=-1, keepdims=True) + EPS) * g


def _silu(z):
    return z * jax.nn.sigmoid(z)


def _norm_proj_kernel(x_ref, g_ref, w_ref, o_ref, hn_ref):
    @pl.when(pl.program_id(1) == 0)
    def _():
        hn_ref[...] = _rms(x_ref[...], g_ref[...]).astype(BF16)

    o_ref[...] = jnp.dot(hn_ref[...], w_ref[...], preferred_element_type=F32)


def norm_proj(x, g, w, *, tm, tn=512):
    m, d = x.shape
    n = w.shape[1]
    tm = min(tm, m)
    return pl.pallas_call(
        _norm_proj_kernel,
        grid=(m // tm, n // tn),
        in_specs=[pl.BlockSpec((tm, d), lambda i, j: (i, 0)),
                  pl.BlockSpec((1, d), lambda i, j: (0, 0)),
                  pl.BlockSpec((d, tn), lambda i, j: (0, j))],
        out_specs=pl.BlockSpec((tm, tn), lambda i, j: (i, j)),
        out_shape=jax.ShapeDtypeStruct((m, n), F32),
        scratch_shapes=[pltpu.VMEM((tm, d), BF16)],
        compiler_params=_params("parallel", "arbitrary"),
        name="norm_proj",
    )(x, g.reshape(1, d), w)


def _mem_kv_kernel(x_ref, g_ref, w_ref, k_ref, v_ref, kb_ref, vb_ref):
    hn = _rms(x_ref[...], g_ref[0]).astype(BF16)
    kv = jnp.dot(hn, w_ref[0], preferred_element_type=F32)
    k, v = kv[:, :XA_DIM], kv[:, XA_DIM:]
    k_ref[0] = k
    v_ref[0] = v
    kb_ref[0] = k.astype(BF16)
    vb_ref[0] = v.astype(BF16)


def mem_kv(mem, g, w, *, tm=512):
    m, d = mem.shape
    depth = g.shape[0]
    out = pl.BlockSpec((1, tm, XA_DIM), lambda l, i: (l, i, 0))
    shape = lambda dt: jax.ShapeDtypeStruct((depth, m, XA_DIM), dt)
    return pl.pallas_call(
        _mem_kv_kernel,
        grid=(depth, m // tm),
        in_specs=[pl.BlockSpec((tm, d), lambda l, i: (i, 0)),
                  pl.BlockSpec((1, 1, d), lambda l, i: (l, 0, 0)),
                  pl.BlockSpec((1, d, 2 * XA_DIM), lambda l, i: (l, 0, 0))],
        out_specs=[out, out, out, out],
        out_shape=[shape(F32), shape(F32), shape(BF16), shape(BF16)],
        compiler_params=_params("parallel", "parallel"),
        name="mem_kv",
    )(mem, g.reshape(depth, 1, d), w)


def _attn_kernel(q_ref, k_ref, v_ref, o_ref, *, nb):
    scale = XA_HEAD_DIM ** -0.5
    for b in range(nb):
        q = q_ref[b].astype(BF16)
        k = k_ref[b].astype(BF16)
        v = v_ref[b].astype(BF16)
        for h in range(XA_HEADS):
            hs = slice(h * XA_HEAD_DIM, (h + 1) * XA_HEAD_DIM)
            s = lax.dot_general(q[:, hs], k[:, hs], (((1,), (1,)), ((), ())),
                                preferred_element_type=F32) * scale
            e = jnp.exp(s - jnp.max(s, axis=-1, keepdims=True))
            p = e / jnp.sum(e, axis=-1, keepdims=True)
            o = jnp.dot(p.astype(BF16), v[:, hs], preferred_element_type=F32)
            o_ref[b, :, hs] = o.astype(o_ref.dtype)


def cross_attention(q, qcol, k, v, *, nb, tq, out_dtype):
    b, l, _ = q.shape
    mem = k.shape[1]
    return pl.pallas_call(
        functools.partial(_attn_kernel, nb=nb),
        grid=(b // nb, l // tq),
        in_specs=[pl.BlockSpec((nb, tq, XA_DIM), lambda i, j: (i, j, qcol)),
                  pl.BlockSpec((nb, mem, XA_DIM), lambda i, j: (i, 0, 0)),
                  pl.BlockSpec((nb, mem, XA_DIM), lambda i, j: (i, 0, 0))],
        out_specs=pl.BlockSpec((nb, tq, XA_DIM), lambda i, j: (i, j, 0)),
        out_shape=jax.ShapeDtypeStruct((b, l, XA_DIM), out_dtype),
        compiler_params=_params("parallel", "arbitrary"),
        name="cross_attention",
    )(q, k, v)


def _attn_sample_kernel(q_ref, k_ref, v_ref, bias_ref, o_ref, *, nb):
    scale = XA_HEAD_DIM ** -0.5
    for b in range(nb):
        q = q_ref[b].astype(BF16)
        k = k_ref[b].astype(BF16)
        v = v_ref[b].astype(BF16)
        s = lax.dot_general(q, k, (((1,), (1,)), ((), ())), preferred_element_type=F32) * scale + bias_ref[...]
        e = jnp.exp(s - jnp.max(s, axis=-1, keepdims=True))
        p = e / jnp.sum(e, axis=-1, keepdims=True)
        o_ref[b] = jnp.dot(p.astype(BF16), v, preferred_element_type=F32)


def cross_attention_sample(q, cache_k, cache_v, bias, layer, *, nb=8):
    b, nq, dh = q.shape
    nkv = cache_k.shape[2]
    kv = pl.BlockSpec((None, nb, nkv, dh), lambda i: (layer, i, 0, 0))
    return pl.pallas_call(
        functools.partial(_attn_sample_kernel, nb=nb),
        grid=(b // nb,),
        in_specs=[pl.BlockSpec((nb, nq, dh), lambda i: (i, 0, 0)), kv, kv, _resident((nq, nkv))],
        out_specs=pl.BlockSpec((nb, nq, dh), lambda i: (i, 0, 0)),
        out_shape=jax.ShapeDtypeStruct((b, nq, dh), F32),
        compiler_params=_params("parallel"),
        name="cross_attention_sample",
    )(q, cache_k, cache_v, bias)


def _out_proj_kernel(y_ref, xa_ref, w1_ref, w2_ref, x_ref, fg_ref, o_ref, *, final_norm):
    acc = x_ref[...] + jnp.dot(y_ref[...], w1_ref[...], preferred_element_type=F32)
    acc = acc + jnp.dot(xa_ref[...], w2_ref[...], preferred_element_type=F32)
    o_ref[...] = _rms(acc, fg_ref[...]) if final_norm else acc


def out_proj(y, xa, w1, w2, x, final_g, *, final_norm, tm=512):
    m, d = x.shape
    e, a = y.shape[1], xa.shape[1]
    tm = min(tm, m)
    return pl.pallas_call(
        functools.partial(_out_proj_kernel, final_norm=final_norm),
        grid=(m // tm,),
        in_specs=[pl.BlockSpec((tm, e), lambda i: (i, 0)),
                  pl.BlockSpec((tm, a), lambda i: (i, 0)),
                  _resident((e, d)),
                  _resident((a, d)),
                  pl.BlockSpec((tm, d), lambda i: (i, 0)),
                  _resident((1, d))],
        out_specs=pl.BlockSpec((tm, d), lambda i: (i, 0)),
        out_shape=jax.ShapeDtypeStruct((m, d), F32),
        compiler_params=_params("parallel"),
        name="out_proj",
    )(y, xa, w1, w2, x, final_g.reshape(1, d))


def _sgu_sample_kernel(u_ref, v_ref, z_ref, g_ref, coef_ref, bias_ref, o_ref, vn_ref):
    steps = u_ref.shape[0]
    vn = []
    for t in range(steps):
        n = _rms(v_ref[t], g_ref[...])
        vn_ref[t] = n
        vn.append(n)
    for t in range(steps):
        mixed = bias_ref[t:t + 1, :]
        for s in range(t + 1):
            mixed = mixed + coef_ref[t * steps + s:t * steps + s + 1, :] * vn[s]
        o_ref[t] = (u_ref[t] * mixed * _silu(z_ref[t])).astype(BF16)


def sgu_sample(proj, g, coef, bias, *, e, nbb=32):
    t, b, _ = proj.shape
    col = lambda c: pl.BlockSpec((t, nbb, e), lambda i: (0, i, c))
    out = pl.BlockSpec((t, nbb, e), lambda i: (0, i, 0))
    return pl.pallas_call(
        _sgu_sample_kernel,
        grid=(b // nbb,),
        in_specs=[col(0), col(1), col(2), _resident((1, e)), _resident(coef.shape), _resident(bias.shape)],
        out_specs=[out, out],
        out_shape=[jax.ShapeDtypeStruct((t, b, e), BF16), jax.ShapeDtypeStruct((t, b, e), F32)],
        compiler_params=_params("parallel"),
        name="sgu_sample",
    )(proj, proj, proj, g.reshape(1, e), coef, bias)


def _conv_sample_kernel(bg_ref, cg_ref, hv_ref, z_ref, w_ref, st_ref, o_ref, ns_ref):
    steps = cg_ref.shape[0]
    xp = [st_ref[k] for k in range(CONV_W - 1)] + [cg_ref[t] * hv_ref[t] for t in range(steps)]
    for t in range(steps):
        y = w_ref[0:1, :] * xp[t]
        for k in range(1, CONV_W):
            y = y + w_ref[k:k + 1, :] * xp[t + k]
        o_ref[t] = (bg_ref[t] * y * _silu(z_ref[t])).astype(BF16)
    for k in range(CONV_W - 1):
        ns_ref[k] = xp[steps + k]


def conv_sample(proj, w, state, *, e, nbb=32):
    t, b, _ = proj.shape
    col = lambda c: pl.BlockSpec((t, nbb, e), lambda i: (0, i, c))
    st = pl.BlockSpec((CONV_W - 1, nbb, e), lambda i: (0, i, 0))
    return pl.pallas_call(
        _conv_sample_kernel,
        grid=(b // nbb,),
        in_specs=[col(0), col(1), col(2), col(3), _resident((CONV_W, e)), st],
        out_specs=[pl.BlockSpec((t, nbb, e), lambda i: (0, i, 0)), st],
        out_shape=[jax.ShapeDtypeStruct((t, b, e), BF16), jax.ShapeDtypeStruct((CONV_W - 1, b, e), F32)],
        compiler_params=_params("parallel"),
        name="conv_sample",
    )(proj, proj, proj, proj, w, state)


LAYER_COLS = 256


def _attend_into(q, k_ref, v_ref, y_ref, col0):
    scale = XA_HEAD_DIM ** -0.5
    qb = q.astype(BF16)
    for h in range(XA_HEADS):
        hs = slice(h * XA_HEAD_DIM, (h + 1) * XA_HEAD_DIM)
        s = lax.dot_general(qb[:, hs], k_ref[:, hs], (((1,), (1,)), ((), ())),
                            preferred_element_type=F32) * scale
        ex = jnp.exp(s - jnp.max(s, axis=-1, keepdims=True))
        p = ex / jnp.sum(ex, axis=-1, keepdims=True)
        o = jnp.dot(p.astype(BF16), v_ref[:, hs], preferred_element_type=F32)
        y_ref[:, col0 + h * XA_HEAD_DIM:col0 + (h + 1) * XA_HEAD_DIM] = o.astype(BF16)


def _residual_out(x, y_ref, wo_ref, fg_ref, o_ref, final_norm):
    acc = x + jnp.dot(y_ref[...], wo_ref[...], preferred_element_type=F32)
    o_ref[...] = _rms(acc, fg_ref[...]) if final_norm else acc


def _sgu_layer_kernel(x_ref, ng_ref, w_ref, sg_ref, ws_ref, bias_ref, k_ref, v_ref, wo_ref, fg_ref, o_ref,
                      hn_ref, vn_ref, y_ref, *, e, final_norm):
    tm = x_ref.shape[0]
    gd = e // SGU_GROUPS
    x = x_ref[...]
    hn_ref[...] = _rms(x, ng_ref[...]).astype(BF16)
    hn = hn_ref[...]

    for c0 in range(0, e, LAYER_COLS):
        vn_ref[:, c0:c0 + LAYER_COLS] = jnp.dot(hn, w_ref[:, e + c0:e + c0 + LAYER_COLS],
                                                preferred_element_type=F32)
    vn_ref[...] = _rms(vn_ref[...], sg_ref[...])

    row = lax.broadcasted_iota(jnp.int32, (CHUNK, CHUNK), 0)
    col = lax.broadcasted_iota(jnp.int32, (CHUNK, CHUNK), 1)
    for g in range(SGU_GROUPS):
        cs = slice(g * gd, (g + 1) * gd)
        wg = jnp.where(row >= col, ws_ref[g], 0.0).astype(BF16)
        u = jnp.dot(hn, w_ref[:, g * gd:(g + 1) * gd], preferred_element_type=F32)
        z = jnp.dot(hn, w_ref[:, 2 * e + g * gd:2 * e + (g + 1) * gd], preferred_element_type=F32)
        for c in range(tm // CHUNK):
            rs = slice(c * CHUNK, (c + 1) * CHUNK)
            mixed = jnp.dot(wg, vn_ref[rs, cs].astype(BF16), preferred_element_type=F32) + bias_ref[:, cs]
            y_ref[rs, cs] = (u[rs] * mixed * _silu(z[rs])).astype(BF16)

    q = jnp.dot(hn, w_ref[:, 3 * e:3 * e + XA_DIM], preferred_element_type=F32)
    _attend_into(q, k_ref, v_ref, y_ref, e)
    _residual_out(x, y_ref, wo_ref, fg_ref, o_ref, final_norm)


def sgu_layer(x, ng, w, sg, ws, bias, kb, vb, layer, wo, fg, *, b, e, final_norm, tm=512):
    m, d = x.shape
    lt = m // b // tm
    mem = kb.shape[2]
    kv = pl.BlockSpec((None, None, mem, XA_DIM), lambda bi, i: (layer, bi, 0, 0))
    return pl.pallas_call(
        functools.partial(_sgu_layer_kernel, e=e, final_norm=final_norm),
        grid=(b, lt),
        in_specs=[pl.BlockSpec((tm, d), lambda bi, i: (bi * lt + i, 0)),
                  _resident((1, d)), _resident(w.shape), _resident((1, e)), _resident(ws.shape),
                  _resident(bias.shape), kv, kv, _resident(wo.shape), _resident((1, d))],
        out_specs=pl.BlockSpec((tm, d), lambda bi, i: (bi * lt + i, 0)),
        out_shape=jax.ShapeDtypeStruct((m, d), F32),
        scratch_shapes=[pltpu.VMEM((tm, d), BF16), pltpu.VMEM((tm, e), F32),
                        pltpu.VMEM((tm, e + XA_DIM), BF16)],
        compiler_params=_params("parallel", "arbitrary"),
        name="sgu_layer",
    )(x, ng.reshape(1, d), w, sg.reshape(1, e), ws, bias, kb, vb, wo, fg.reshape(1, d))


def _conv_layer_kernel(x_ref, ng_ref, w_ref, cw_ref, init_ref, k_ref, v_ref, wo_ref, fg_ref, o_ref, st_ref,
                       hn_ref, y_ref, carry_ref, *, e, final_norm):
    tm = x_ref.shape[0]
    keep = CONV_W - 1
    edge = carry_ref.shape[0]

    @pl.when(pl.program_id(1) == 0)
    def _():
        carry_ref[...] = jnp.zeros_like(carry_ref)
        carry_ref[edge - keep:edge, :] = init_ref[...]

    x = x_ref[...]
    hn_ref[...] = _rms(x, ng_ref[...]).astype(BF16)
    hn = hn_ref[...]
    rid = lax.broadcasted_iota(jnp.int32, (tm, LAYER_COLS), 0)
    for c0 in range(0, e, LAYER_COLS):
        cs = slice(c0, c0 + LAYER_COLS)
        part = lambda k: jnp.dot(hn, w_ref[:, k * e + c0:k * e + c0 + LAYER_COLS], preferred_element_type=F32)
        xc = part(1) * part(2)
        p1 = carry_ref[edge - 1:edge, cs]
        p2 = carry_ref[edge - 2:edge - 1, cs]
        x1 = jnp.where(rid == 0, p1, pltpu.roll(xc, 1, 0))
        x2 = jnp.where(rid == 0, p2, jnp.where(rid == 1, p1, pltpu.roll(xc, 2, 0)))
        y = cw_ref[0:1, cs] * x2 + cw_ref[1:2, cs] * x1 + cw_ref[2:3, cs] * xc
        y_ref[:, cs] = (part(0) * y * _silu(part(3))).astype(BF16)
        carry_ref[:, cs] = xc[tm - edge:, :]
        st_ref[:, cs] = xc[tm - keep:, :]

    q = jnp.dot(hn, w_ref[:, 4 * e:4 * e + XA_DIM], preferred_element_type=F32)
    _attend_into(q, k_ref, v_ref, y_ref, e)
    _residual_out(x, y_ref, wo_ref, fg_ref, o_ref, final_norm)


def conv_layer(x, ng, w, cw, init, kb, vb, layer, wo, fg, *, b, e, final_norm, tm=512):
    m, d = x.shape
    lt = m // b // tm
    mem = kb.shape[2]
    kv = pl.BlockSpec((None, None, mem, XA_DIM), lambda bi, i: (layer, bi, 0, 0))
    st = pl.BlockSpec((None, CONV_W - 1, e), lambda bi, i: (bi, 0, 0))
    return pl.pallas_call(
        functools.partial(_conv_layer_kernel, e=e, final_norm=final_norm),
        grid=(b, lt),
        in_specs=[pl.BlockSpec((tm, d), lambda bi, i: (bi * lt + i, 0)),
                  _resident((1, d)), _resident(w.shape), _resident((CONV_W, e)), st,
                  kv, kv, _resident(wo.shape), _resident((1, d))],
        out_specs=[pl.BlockSpec((tm, d), lambda bi, i: (bi * lt + i, 0)), st],
        out_shape=[jax.ShapeDtypeStruct((m, d), F32), jax.ShapeDtypeStruct((b, CONV_W - 1, e), F32)],
        scratch_shapes=[pltpu.VMEM((tm, d), BF16), pltpu.VMEM((tm, e + XA_DIM), BF16),
                        pltpu.VMEM((V7X_SUBLANES, e), F32)],
        compiler_params=_params("parallel", "arbitrary"),
        name="conv_layer",
    )(x, ng.reshape(1, d), w, cw, init, kb, vb, wo, fg.reshape(1, d))


def _s5_disc_kernel(lre_ref, lim_ref, ldt_ref, bre_ref, bim_ref, ar_ref, ai_ref, bbr_ref, bbi_ref):
    dt = jnp.exp(ldt_ref[...])
    lr, li = lre_ref[...], lim_ref[...]
    mag = jnp.exp(lr * dt)
    ar = mag * jnp.cos(li * dt)
    ai = mag * jnp.sin(li * dt)
    nr, ni = ar - 1, ai
    den = lr * lr + li * li
    cr = (nr * lr + ni * li) / den
    ci = (ni * lr - nr * li) / den
    ar_ref[...] = ar
    ai_ref[...] = ai
    cr3, ci3 = cr[:, None, :], ci[:, None, :]
    bbr_ref[...] = cr3 * bre_ref[...] - ci3 * bim_ref[...]
    bbi_ref[...] = cr3 * bim_ref[...] + ci3 * bre_ref[...]


def s5_discretize(lam_re, lam_im, log_dt, b_re, b_im):
    g, p = lam_re.shape
    gp = jax.ShapeDtypeStruct((g, p), F32)
    gip = jax.ShapeDtypeStruct(b_re.shape, F32)
    return pl.pallas_call(
        _s5_disc_kernel,
        out_shape=[gp, gp, gip, gip],
        compiler_params=pltpu.CompilerParams(vmem_limit_bytes=V7X_VMEM_LIMIT_BYTES),
        name="s5_discretize",
    )(lam_re, lam_im, log_dt.reshape(g, 1), b_re, b_im)


def _s5_kernel(u_ref, bre_ref, bim_ref, cre_ref, cim_ref, ar_ref, ai_ref, d_ref, h0r_ref, h0i_ref,
               g_ref, hr_ref, hi_ref, sr_ref, si_ref):
    tl, nb, e = u_ref.shape
    rows = tl * nb
    n_blk = bre_ref.shape[0]
    fk = bre_ref.shape[1]
    sk = bre_ref.shape[2]
    lanes = sr_ref.shape[1]

    @pl.when(pl.program_id(1) == 0)
    def _():
        hr_ref[...] = h0r_ref[...]
        hi_ref[...] = h0i_ref[...]

    u = u_ref[...].reshape(rows, e)
    ub = u.astype(BF16)
    for kb in range(n_blk):
        bs = slice(kb * sk, (kb + 1) * sk)
        fs = slice(kb * fk, (kb + 1) * fk)
        sr_ref[:, bs] = jnp.dot(ub[:, fs], bre_ref[kb], preferred_element_type=F32)
        si_ref[:, bs] = jnp.dot(ub[:, fs], bim_ref[kb], preferred_element_type=F32)

        for c0 in range(kb * sk, (kb + 1) * sk, S5_SCAN_LANES):
            ls = slice(c0, c0 + S5_SCAN_LANES)
            a_r = jnp.broadcast_to(ar_ref[:, ls], (V7X_SUBLANES, S5_SCAN_LANES))
            a_i = jnp.broadcast_to(ai_ref[:, ls], (V7X_SUBLANES, S5_SCAN_LANES))
            for r0 in range(0, nb, V7X_SUBLANES):
                hr = hr_ref[r0:r0 + V7X_SUBLANES, ls]
                hi = hi_ref[r0:r0 + V7X_SUBLANES, ls]
                for t in range(tl):
                    rs = slice(t * nb + r0, t * nb + r0 + V7X_SUBLANES)
                    hr, hi = (a_r * hr - a_i * hi + sr_ref[rs, ls],
                              a_r * hi + a_i * hr + si_ref[rs, ls])
                    sr_ref[rs, ls] = hr
                    si_ref[rs, ls] = hi
                hr_ref[r0:r0 + V7X_SUBLANES, ls] = hr
                hi_ref[r0:r0 + V7X_SUBLANES, ls] = hi

        y = jnp.dot(sr_ref[:, bs].astype(BF16), cre_ref[kb], preferred_element_type=F32)
        y = y - jnp.dot(si_ref[:, bs].astype(BF16), cim_ref[kb], preferred_element_type=F32)
        y = y + d_ref[:, fs] * u[:, fs]
        g_ref[:, :, fs] = jax.nn.gelu(y).reshape(tl, nb, fk)


def s5_scan(proj, bre, bim, cre, cim, ar, ai, d, h0r, h0i, *, e, tl, nbb):
    l, b, _ = proj.shape
    lanes = h0r.shape[1]
    rows = tl * nbb
    st = pl.BlockSpec((nbb, lanes), lambda bi, i: (bi, 0))
    return pl.pallas_call(
        _s5_kernel,
        grid=(b // nbb, l // tl),
        in_specs=[pl.BlockSpec((tl, nbb, e), lambda bi, i: (i, bi, 0)),
                  _resident(bre.shape), _resident(bim.shape), _resident(cre.shape), _resident(cim.shape),
                  _resident((1, lanes)), _resident((1, lanes)), _resident((1, e)), st, st],
        out_specs=[pl.BlockSpec((tl, nbb, e), lambda bi, i: (i, bi, 0)), st, st],
        out_shape=[jax.ShapeDtypeStruct((l, b, e), F32),
                   jax.ShapeDtypeStruct((b, lanes), F32), jax.ShapeDtypeStruct((b, lanes), F32)],
        scratch_shapes=[pltpu.VMEM((rows, lanes), F32), pltpu.VMEM((rows, lanes), F32)],
        compiler_params=_params("parallel", "arbitrary"),
        name="s5_scan",
    )(proj, bre, bim, cre, cim, ar.reshape(1, lanes), ai.reshape(1, lanes), d.reshape(1, e), h0r, h0i)


GLU_COLS = 512


def _glu_kernel(g_ref, w_ref, b_ref, z_ref, o_ref, gb_ref):
    gb_ref[...] = g_ref[...].astype(BF16)
    for c0 in range(0, o_ref.shape[1], GLU_COLS):
        cs = slice(c0, c0 + GLU_COLS)
        a = jnp.dot(gb_ref[...], w_ref[:, cs], preferred_element_type=F32) + b_ref[:, cs]
        o_ref[:, cs] = (g_ref[:, cs] * jax.nn.sigmoid(a) * _silu(z_ref[:, cs])).astype(BF16)


def glu_gate(g, w, bias, proj, *, e, tm=512):
    m = g.shape[0]
    tm = min(tm, m)
    return pl.pallas_call(
        _glu_kernel,
        grid=(m // tm,),
        in_specs=[pl.BlockSpec((tm, e), lambda i: (i, 0)),
                  _resident((e, e)),
                  _resident((1, e)),
                  pl.BlockSpec((tm, e), lambda i: (i, 1))],
        out_specs=pl.BlockSpec((tm, e), lambda i: (i, 0)),
        out_shape=jax.ShapeDtypeStruct((m, e), BF16),
        scratch_shapes=[pltpu.VMEM((tm, e), BF16)],
        compiler_params=_params("parallel"),
        name="glu_gate",
    )(g, w, bias.reshape(1, e), proj)


def _block_diag(x, n_blk):
    g, r, c = x.shape
    gb = g // n_blk
    eye = jnp.eye(gb, dtype=x.dtype)
    out = jnp.einsum("bgrc,gh->bgrhc", x.reshape(n_blk, gb, r, c), eye)
    return out.reshape(n_blk, gb * r, gb * c)


def kernel(x_prompt, x_sample, mem_prompt, cache_mem_k, cache_mem_v, state_conv, state_s5_re, state_s5_im,
           norm_g, final_g, mem_norm_g, w_kv, w_out, w_in_a, sgu_norm_g, sgu_w, sgu_b, w_in_b, conv_w, w_in_c,
           s5_lam_re, s5_lam_im, s5_log_dt, s5_b_re, s5_b_im, s5_c_re, s5_c_im, s5_d, w_glu, b_glu):
    bp, lp, d = x_prompt.shape
    bs, ls, _ = x_sample.shape
    depth = norm_g.shape[0]
    e = conv_w.shape[-1]
    mem_len = mem_prompt.shape[1]
    n_grp, n_state = s5_lam_re.shape[1:]
    gd = e // SGU_GROUPS
    n_blk = e // S5_FEATS_PER_BLOCK
    kinds = [i % N_MIXERS for i in range(depth)]
    kidx = [sum(1 for j in range(i) if kinds[j] == kinds[i]) for i in range(depth)]

    mk, mv, mkb, mvb = mem_kv(mem_prompt.reshape(bp * mem_len, d), mem_norm_g, w_kv.astype(BF16))
    mem_k_prompt = mk.reshape(depth, bp, mem_len, XA_HEADS, XA_HEAD_DIM)
    mem_v_prompt = mv.reshape(depth, bp, mem_len, XA_HEADS, XA_HEAD_DIM)
    mkb = mkb.reshape(depth, bp, mem_len, XA_DIM)
    mvb = mvb.reshape(depth, bp, mem_len, XA_DIM)
    ck = cache_mem_k.reshape(depth, bs, mem_len * XA_HEADS, XA_HEAD_DIM)
    cv = cache_mem_v.reshape(depth, bs, mem_len * XA_HEADS, XA_HEAD_DIM)
    q_head = jnp.arange(XA_HEADS * ls, dtype=jnp.int32)[:, None] // ls
    kv_head = jnp.arange(mem_len * XA_HEADS, dtype=jnp.int32)[None, :] % XA_HEADS
    head_bias = jnp.where(q_head == kv_head, 0.0, -1e30).astype(F32)

    xp = x_prompt.reshape(bp * lp, d)
    xs = x_sample.transpose(1, 0, 2).reshape(ls * bs, d)

    def sample_attention(proj, qcol, i):
        q = proj.reshape(ls, bs, -1)[:, :, qcol * XA_DIM:(qcol + 1) * XA_DIM]
        q = q.reshape(ls, bs, XA_HEADS, XA_HEAD_DIM).transpose(1, 2, 0, 3).reshape(bs, XA_HEADS * ls, XA_HEAD_DIM)
        xa = cross_attention_sample(q, ck, cv, head_bias, i)
        xa = xa.reshape(bs, XA_HEADS, ls, XA_HEAD_DIM).transpose(2, 0, 1, 3)
        return xa.reshape(ls * bs, XA_DIM).astype(BF16)

    conv_p, conv_s, s5r_p, s5i_p, s5r_s, s5i_s, chunk_v = [], [], [], [], [], [], []
    for i in range(depth):
        kind, j = kinds[i], kidx[i]
        wo = w_out[i].astype(BF16)
        w1, w2 = wo[:e], wo[e:]
        last = i == depth - 1
        if kind == 0:
            w = w_in_a[j].astype(BF16)
            qcol = 3 * e // XA_DIM
            bias = jnp.repeat(sgu_b[j].T, gd, axis=1)
            xp = sgu_layer(xp, norm_g[i], w, sgu_norm_g[j], sgu_w[j], bias, mkb, mvb, i, wo, final_g,
                           b=bp, e=e, final_norm=last)
            ps = norm_proj(xs, norm_g[i], w, tm=512)
            coef = jnp.repeat(sgu_w[j][:, :ls, :ls].transpose(1, 2, 0).reshape(ls * ls, SGU_GROUPS), gd, axis=1)
            ys, vn = sgu_sample(ps.reshape(ls, bs, -1), sgu_norm_g[j], coef, bias[:V7X_SUBLANES], e=e)
            chunk_v.append(vn.transpose(1, 0, 2))
            ys = ys.reshape(ls * bs, e)
            as_ = sample_attention(ps, qcol, i)
        elif kind == 1:
            w = w_in_b[j].astype(BF16)
            qcol = 4 * e // XA_DIM
            xp, cst = conv_layer(xp, norm_g[i], w, conv_w[j], jnp.zeros((bp, CONV_W - 1, e), F32), mkb, mvb, i,
                                 wo, final_g, b=bp, e=e, final_norm=last)
            conv_p.append(cst)
            ps = norm_proj(xs, norm_g[i], w, tm=512)
            ys, nst = conv_sample(ps.reshape(ls, bs, -1), conv_w[j], state_conv[j].transpose(1, 0, 2), e=e)
            conv_s.append(nst.transpose(1, 0, 2))
            ys = ys.reshape(ls * bs, e)
            as_ = sample_attention(ps, qcol, i)
        else:
            wc = w_in_c[j].astype(BF16)
            w_uz, w_q = wc[:, :2 * e], wc[:, 2 * e:]
            ar, ai, bbr, bbi = s5_discretize(s5_lam_re[j], s5_lam_im[j], s5_log_dt[j],
                                             s5_b_re[j].transpose(0, 2, 1), s5_b_im[j].transpose(0, 2, 1))
            bre = _block_diag(bbr, n_blk).astype(BF16)
            bim = _block_diag(bbi, n_blk).astype(BF16)
            cre = _block_diag(s5_c_re[j].transpose(0, 2, 1), n_blk).astype(BF16)
            cim = _block_diag(s5_c_im[j].transpose(0, 2, 1), n_blk).astype(BF16)
            ar, ai = ar.reshape(-1), ai.reshape(-1)

            xt = xp.reshape(bp, lp, d).transpose(1, 0, 2).reshape(lp * bp, d)
            pt = norm_proj(xt, norm_g[i], w_uz, tm=2048, tn=1024)
            pq = norm_proj(xp, norm_g[i], w_q, tm=2048)
            zero = jnp.zeros((bp, n_grp * n_state), F32)
            gp, hr, hi = s5_scan(pt.reshape(lp, bp, -1), bre, bim, cre, cim, ar, ai, s5_d[j], zero, zero,
                                 e=e, tl=32, nbb=bp)
            s5r_p.append(hr.reshape(bp, n_grp, n_state))
            s5i_p.append(hi.reshape(bp, n_grp, n_state))
            yt = glu_gate(gp.reshape(lp * bp, e), w_glu[j].astype(BF16), b_glu[j], pt, e=e)
            ap = cross_attention(pq.reshape(bp, lp, -1), 0, mkb[i], mvb[i], nb=1, tq=512, out_dtype=BF16)
            at = ap.transpose(1, 0, 2).reshape(lp * bp, XA_DIM)
            xt = out_proj(yt, at, w1, w2, xt, final_g, final_norm=last)
            xp = xt.reshape(lp, bp, d).transpose(1, 0, 2).reshape(bp * lp, d)

            ps = norm_proj(xs, norm_g[i], wc, tm=512)
            gs, hr, hi = s5_scan(ps.reshape(ls, bs, -1), bre, bim, cre, cim, ar, ai, s5_d[j],
                                 state_s5_re[j].reshape(bs, -1), state_s5_im[j].reshape(bs, -1),
                                 e=e, tl=ls, nbb=32)
            s5r_s.append(hr.reshape(bs, n_grp, n_state))
            s5i_s.append(hi.reshape(bs, n_grp, n_state))
            ys = glu_gate(gs.reshape(ls * bs, e), w_glu[j].astype(BF16), b_glu[j], ps, e=e)
            as_ = sample_attention(ps, 2 * e // XA_DIM, i)
        xs = out_proj(ys, as_, w1, w2, xs, final_g, final_norm=last)

    y_prompt = xp.reshape(bp, lp, d)
    y_sample = xs.reshape(ls, bs, d).transpose(1, 0, 2)
    return (y_prompt, y_sample, mem_k_prompt, mem_v_prompt, jnp.stack(conv_p), jnp.stack(conv_s),
            jnp.stack(s5r_p), jnp.stack(s5i_p), jnp.stack(s5r_s), jnp.stack(s5i_s), jnp.stack(chunk_v))
```
